```python
import jax, jax.numpy as jnp
from jax import lax
import numpy as np

D_MODEL = 1024
BATCH = 8
SEQ = 4096
DEPTH = 1
DEC_BATCH = 16
DEC_SEQ = 32
PAST_LEN = 4096

CHUNK = 64
MIX_WIDTH = D_MODEL
HG_WIDTH = MIX_WIDTH // 2
HG_HEADS = 4
HG_DIM = HG_WIDTH // HG_HEADS
HG_BLOCK = 16
HG_COLS = 4 * HG_WIDTH
RW_WIDTH = MIX_WIDTH - HG_WIDTH
RW_HEAD = 64
RW_HEADS = RW_WIDTH // RW_HEAD
RW_DECAY_LORA = 64
RW_A_LORA = 64
RW_GATE_LORA = 128
RW_COLS = 3 * RW_WIDTH + RW_DECAY_LORA + RW_A_LORA + RW_GATE_LORA
IN_COLS = HG_COLS + RW_COLS
N_MEM = 256
X_HEADS = 4
X_DIM = D_MODEL // X_HEADS
D_FF = -(-8 * D_MODEL // (3 * 256)) * 256
RMS_EPS = 1e-6
GN_EPS = 64e-5

kernel_name = "hgrn2_rwkv7_parallel_heads_stream_step"


def rmsnorm(x, g):
    xf = x.astype(jnp.float32)
    y = xf * lax.rsqrt(jnp.mean(xf * xf, axis=-1, keepdims=True) + RMS_EPS)
    return (y * g.astype(jnp.float32)).astype(x.dtype)


def hgrn2_chunkwise(q, logf, k, v, s0):
    bsz, t_len = q.shape[0], q.shape[1]
    pad = (-t_len) % HG_BLOCK

    def blocks(a):
        a = jnp.pad(a, ((0, 0), (0, pad), (0, 0), (0, 0)))
        return a.reshape(bsz, -1, HG_BLOCK, HG_HEADS, HG_DIM)

    q, logf, k, v = blocks(q), blocks(logf), blocks(k), blocks(v)
    b = jnp.cumsum(logf, axis=2)
    b_last = b[:, :, -1:]
    qg = q * jnp.exp(b)
    kg = k * jnp.exp(-b)
    kd = k * jnp.exp(b_last - b)
    causal = jnp.tril(jnp.ones((HG_BLOCK, HG_BLOCK), dtype=bool))
    att = jnp.where(causal, jnp.einsum('bnthk,bnshk->bnhts', qg, kg), 0.0)
    o_intra = jnp.einsum('bnhts,bnshv->bnthv', att, v)

    def step(S, xs):
        qg_n, kd_n, v_n, dec_n = xs
        o_n = jnp.einsum('bthk,bhkv->bthv', qg_n, S)
        S = dec_n[..., None] * S + jnp.einsum('bshk,bshv->bhkv', kd_n, v_n)
        return S, o_n

    blk_first = lambda a: jnp.moveaxis(a, 1, 0)
    s_final, o_inter = lax.scan(step, s0, (blk_first(qg), blk_first(kd), blk_first(v),
                                           blk_first(jnp.exp(b_last[:, :, 0]))))
    o = (o_intra + blk_first(o_inter)).reshape(bsz, -1, HG_HEADS, HG_DIM)[:, :t_len]
    return o, s_final


def hgrn2_mixer(p, lb, g_norm, s0):
    bsz, t_len = p.shape[0], p.shape[1]
    pf = p.astype(jnp.float32)
    q, fpre, i_in, g = jnp.split(pf, 4, axis=-1)
    lb = lb.astype(jnp.float32)
    logf = jnp.log(lb + (1.0 - lb) * jax.nn.sigmoid(fpre))
    k = (1.0 - lb) * jax.nn.sigmoid(-fpre)
    hv = lambda a: a.reshape(bsz, t_len, HG_HEADS, HG_DIM)
    o, s_final = hgrn2_chunkwise(hv(q), hv(logf), hv(k), hv(i_in), s0.astype(jnp.float32))
    o = o * lax.rsqrt(jnp.mean(o * o, axis=-1, keepdims=True) + RMS_EPS)
    o = o.reshape(bsz, t_len, HG_WIDTH) * g_norm.astype(jnp.float32) * jax.nn.silu(g)
    return o.astype(p.dtype), s_final.astype(p.dtype)


def rwkv7_mixer(p, shift0, s0, mu, w0, w_b, a0, a_b, g_b, k_k, k_a, r_k, gn_w, gn_b):
    bsz, t_len = p.shape[0], p.shape[1]
    prev = jnp.concatenate([shift0.astype(p.dtype), p[:, :-1]], axis=1)
    xs = (p + (prev - p) * mu).astype(jnp.float32)
    new_shift = p[:, -1:]
    cut = [RW_WIDTH, 2 * RW_WIDTH, 3 * RW_WIDTH, 3 * RW_WIDTH + RW_DECAY_LORA,
           3 * RW_WIDTH + RW_DECAY_LORA + RW_A_LORA]
    r, k, v, wd, ad, gd = jnp.split(xs, cut, axis=-1)
    w = -jax.nn.softplus(-(w0 + jnp.tanh(wd) @ w_b)) - 0.5
    decay = jnp.exp(-jnp.exp(w))
    a = jax.nn.sigmoid(a0 + ad @ a_b)
    g = jax.nn.sigmoid(gd) @ g_b
    kk = k * k_k
    k = k * (1.0 + (a - 1.0) * k_a)
    hv = lambda t: t.reshape(bsz, t_len, RW_HEADS, RW_HEAD)
    r, k, v, a, decay, kk = hv(r), hv(k), hv(v), hv(a), hv(decay), hv(kk)
    kk = kk / jnp.maximum(jnp.sqrt(jnp.sum(kk * kk, axis=-1, keepdims=True)), 1e-12)

    def step(S, xs_t):
        r_t, w_t, k_t, v_t, kk_t, a_t = xs_t
        sa = jnp.einsum('bhvk,bhk->bhv', S, -kk_t)
        S = (S * w_t[:, :, None, :] + sa[..., None] * (kk_t * a_t)[:, :, None, :]
             + v_t[..., None] * k_t[:, :, None, :])
        return S, jnp.einsum('bhvk,bhk->bhv', S, r_t)

    tm = lambda t: jnp.moveaxis(t, 1, 0)
    s_final, y = lax.scan(step, s0.astype(jnp.float32), (tm(r), tm(decay), tm(k), tm(v), tm(kk), tm(a)))
    y = jnp.moveaxis(y, 0, 1)
    mean = jnp.mean(y, axis=-1, keepdims=True)
    var = jnp.mean(jnp.square(y - mean), axis=-1, keepdims=True)
    y = ((y - mean) * lax.rsqrt(var + GN_EPS)).reshape(bsz, t_len, RW_WIDTH) * gn_w + gn_b
    bonus = jnp.sum(r * k * r_k, axis=-1, keepdims=True) * v
    y = (y + bonus.reshape(bsz, t_len, RW_WIDTH)) * g
    return y.astype(p.dtype), s_final.astype(p.dtype), new_shift


def memory_kv(mem, g_mem, w_k, w_v):
    bsz = mem.shape[0]
    nm = rmsnorm(mem, g_mem)
    mk = (nm @ w_k).reshape(bsz, N_MEM, X_HEADS, X_DIM)
    mv = (nm @ w_v).reshape(bsz, N_MEM, X_HEADS, X_DIM)
    return mk, mv


def cross_attend(n, mk, mv, w_q, w_o):
    bsz, t_len = n.shape[0], n.shape[1]
    q = (n @ w_q).reshape(bsz, t_len, X_HEADS, X_DIM).astype(jnp.float32)
    s = jnp.einsum('bthd,bmhd->bhtm', q, mk.astype(jnp.float32)) * (X_DIM ** -0.5)
    pr = jax.nn.softmax(s, axis=-1)
    o = jnp.einsum('bhtm,bmhd->bthd', pr, mv.astype(jnp.float32)).astype(n.dtype)
    return o.reshape(bsz, t_len, D_MODEL) @ w_o


def layer(x, mk, mv, hg_s0, rw_s0, shift0, lb, lw):
    n = rmsnorm(x, lw['norm_mix'])
    p = n @ lw['w_in']
    hg_o, hg_s = hgrn2_mixer(p[..., :HG_COLS], lb, lw['hgrn_norm'], hg_s0)
    rw_o, rw_s, rw_shift = rwkv7_mixer(p[..., HG_COLS:], shift0, rw_s0, lw['rw_mu'], lw['rw_w0'], lw['rw_w_b'],
                                       lw['rw_a0'], lw['rw_a_b'], lw['rw_g_b'], lw['rw_k_k'], lw['rw_k_a'],
                                       lw['rw_r_k'], lw['rw_gn_w'], lw['rw_gn_b'])
    x = x + jnp.concatenate([hg_o, rw_o], axis=-1) @ lw['w_out']
    x = x + cross_attend(rmsnorm(x, lw['norm_cross']), mk, mv, lw['w_cq'], lw['w_co'])
    n = rmsnorm(x, lw['norm_ffn'])
    x = x + (jax.nn.silu(n @ lw['w_ff1']) * (n @ lw['w_ff3'])) @ lw['w_ff2']
    return x, hg_s, rw_s, rw_shift


def setup_inputs(seed: int = 0) -> dict:
    key = jax.random.key(seed)
    ks = jax.random.split(key, 40)
    nrm = lambda i, shape, scale: jax.random.normal(ks[i], shape, jnp.float32) * scale
    D = D_MODEL
    return {
        'x_prompt': nrm(0, (BATCH, SEQ, D), 1.0),
        'mem_prompt': nrm(1, (BATCH, N_MEM, D), 1.0),
        'x_sample': nrm(2, (DEC_BATCH, DEC_SEQ, D), 1.0),
        'cache_mem_k': nrm(3, (DEPTH, DEC_BATCH, N_MEM, X_HEADS, X_DIM), 1.0),
        'cache_mem_v': nrm(4, (DEPTH, DEC_BATCH, N_MEM, X_HEADS, X_DIM), 1.0),
        'state_hgrn': nrm(5, (DEPTH, DEC_BATCH, HG_HEADS, HG_DIM, HG_DIM), 0.5),
        'state_rwkv': nrm(6, (DEPTH, DEC_BATCH, RW_HEADS, RW_HEAD, RW_HEAD), 0.5),
        'state_rwkv_shift': nrm(7, (DEPTH, DEC_BATCH, 1, RW_COLS), 1.0),
        'hgrn_lb_logits': nrm(8, (DEPTH + 1, HG_WIDTH), 0.1),
        'norm_mix': 1.0 + nrm(9, (DEPTH, D), 0.02),
        'w_in': nrm(10, (DEPTH, D, IN_COLS), D ** -0.5),
        'hgrn_norm': 1.0 + nrm(11, (DEPTH, HG_WIDTH), 0.02),
        'rw_mu': jax.random.uniform(ks[12], (DEPTH, RW_COLS), jnp.float32),
        'rw_w0': jax.random.uniform(ks[13], (DEPTH, RW_WIDTH), jnp.float32, -2.5, 0.5),
        'rw_w_b': nrm(14, (DEPTH, RW_DECAY_LORA, RW_WIDTH), 0.1),
        'rw_a0': nrm(15, (DEPTH, RW_WIDTH), 0.1),
        'rw_a_b': nrm(16, (DEPTH, RW_A_LORA, RW_WIDTH), 0.1),
        'rw_g_b': nrm(17, (DEPTH, RW_GATE_LORA, RW_WIDTH), RW_GATE_LORA ** -0.5),
        'rw_k_k': 0.85 + nrm(18, (DEPTH, RW_WIDTH), 0.02),
        'rw_k_a': 1.0 + nrm(19, (DEPTH, RW_WIDTH), 0.02),
        'rw_r_k': nrm(20, (DEPTH, RW_HEADS, RW_HEAD), 0.1),
        'rw_gn_w': 1.0 + nrm(21, (DEPTH, RW_WIDTH), 0.02),
        'rw_gn_b': nrm(22, (DEPTH, RW_WIDTH), 0.01),
        'w_out': nrm(23, (DEPTH, MIX_WIDTH, D), MIX_WIDTH ** -0.5),
        'norm_cross': 1.0 + nrm(24, (DEPTH, D), 0.02),
        'norm_mem': 1.0 + nrm(25, (DEPTH, D), 0.02),
        'w_cq': nrm(26, (DEPTH, D, D), D ** -0.5),
        'w_ck': nrm(27, (DEPTH, D, D), D ** -0.5),
        'w_cv': nrm(28, (DEPTH, D, D), D ** -0.5),
        'w_co': nrm(29, (DEPTH, D, D), D ** -0.5),
        'norm_ffn': 1.0 + nrm(30, (DEPTH, D), 0.02),
        'w_ff1': nrm(31, (DEPTH, D, D_FF), D ** -0.5),
        'w_ff3': nrm(32, (DEPTH, D, D_FF), D ** -0.5),
        'w_ff2': nrm(33, (DEPTH, D_FF, D), D_FF ** -0.5),
        'norm_final': 1.0 + nrm(34, (D,), 0.02),
    }


def reference(x_prompt, mem_prompt, x_sample, cache_mem_k, cache_mem_v, state_hgrn, state_rwkv, state_rwkv_shift,
              hgrn_lb_logits, norm_mix, w_in, hgrn_norm, rw_mu, rw_w0, rw_w_b, rw_a0, rw_a_b, rw_g_b, rw_k_k, rw_k_a,
              rw_r_k, rw_gn_w, rw_gn_b, w_out, norm_cross, norm_mem, w_cq, w_ck, w_cv, w_co, norm_ffn, w_ff1, w_ff3,
              w_ff2, norm_final):
    lb_table = jnp.cumsum(jax.nn.softmax(hgrn_lb_logits.astype(jnp.float32), axis=0), axis=0)
    bp = x_prompt.shape[0]
    xp, xs = x_prompt, x_sample
    p_hg, p_rw, p_sh, p_mk, p_mv, s_hg, s_rw, s_sh = [], [], [], [], [], [], [], []
    for l in range(DEPTH):
        lw = {'norm_mix': norm_mix[l], 'w_in': w_in[l], 'hgrn_norm': hgrn_norm[l], 'rw_mu': rw_mu[l],
              'rw_w0': rw_w0[l], 'rw_w_b': rw_w_b[l], 'rw_a0': rw_a0[l], 'rw_a_b': rw_a_b[l], 'rw_g_b': rw_g_b[l],
              'rw_k_k': rw_k_k[l], 'rw_k_a': rw_k_a[l], 'rw_r_k': rw_r_k[l], 'rw_gn_w': rw_gn_w[l],
              'rw_gn_b': rw_gn_b[l], 'w_out': w_out[l], 'norm_cross': norm_cross[l], 'w_cq': w_cq[l],
              'w_co': w_co[l], 'norm_ffn': norm_ffn[l], 'w_ff1': w_ff1[l], 'w_ff3': w_ff3[l], 'w_ff2': w_ff2[l]}
        lb = lb_table[l]
        mk, mv = memory_kv(mem_prompt, norm_mem[l], w_ck[l], w_cv[l])
        hg0 = jnp.zeros((bp, HG_HEADS, HG_DIM, HG_DIM), xp.dtype)
        rw0 = jnp.zeros((bp, RW_HEADS, RW_HEAD, RW_HEAD), xp.dtype)
        sh0 = jnp.zeros((bp, 1, RW_COLS), xp.dtype)
        xp, hg_new, rw_new, sh_new = layer(xp, mk, mv, hg0, rw0, sh0, lb, lw)
        p_hg.append(hg_new); p_rw.append(rw_new); p_sh.append(sh_new); p_mk.append(mk); p_mv.append(mv)
        xs, hg_new, rw_new, sh_new = layer(xs, cache_mem_k[l], cache_mem_v[l], state_hgrn[l], state_rwkv[l],
                                           state_rwkv_shift[l], lb, lw)
        s_hg.append(hg_new); s_rw.append(rw_new); s_sh.append(sh_new)
    y_prompt = rmsnorm(xp, norm_final)
    y_sample = rmsnorm(xs, norm_final)
    return (y_prompt, y_sample, jnp.stack(p_hg), jnp.stack(p_rw), jnp.stack(p_sh), jnp.stack(p_mk),
            jnp.stack(p_mv), jnp.stack(s_hg), jnp.stack(s_rw), jnp.stack(s_sh))
```

```python
import functools

import jax
import jax.numpy as jnp
from jax import lax
from jax.experimental import pallas as pl
from jax.experimental.pallas import tpu as pltpu

F32 = jnp.float32
BF16 = jnp.bfloat16

D_MODEL = 1024
HG_WIDTH = 512
HG_HEADS = 4
HG_DIM = 128
HG_BLOCK = 16
HG_COLS = 4 * HG_WIDTH
RW_WIDTH = 512
RW_HEAD = 64
RW_HEADS = 8
RW_DECAY_LORA = 64
RW_A_LORA = 64
RW_GATE_LORA = 128
RW_COLS = 3 * RW_WIDTH + RW_DECAY_LORA + RW_A_LORA + RW_GATE_LORA
N_MEM = 256
X_HEADS = 4
X_DIM = D_MODEL // X_HEADS
D_FF = 2816
RMS_EPS = 1e-6
GN_EPS = 64e-5

TIME_CHUNK = 64
TOKEN_TILE = 512
FF_CHUNK = 1408
COL_CHUNK = 1024
VMEM_LIMIT_BYTES = 56 * 1024 * 1024

NN = (((1,), (0,)), ((), ()))
NT = (((1,), (1,)), ((), ()))
TN = (((0,), (0,)), ((), ()))


def _bdot(a, b, dims=NN):
    return lax.dot_general(a.astype(BF16), b.astype(BF16), dims, preferred_element_type=F32)


def _fdot(a, b, dims=NN):
    return lax.dot_general(a, b, dims, precision=lax.Precision.HIGHEST, preferred_element_type=F32)


def _segsum(x, j_bf16):
    rows = x.shape[0]
    hi = x.astype(BF16)
    lo = (x - hi.astype(F32)).astype(BF16)
    both = jnp.dot(jnp.concatenate([hi, lo], axis=0), j_bf16, preferred_element_type=F32)
    return both[:rows] + both[rows:]


def _rmsnorm(x, g):
    return x * lax.rsqrt(jnp.mean(x * x, axis=-1, keepdims=True) + RMS_EPS) * g


def _softplus(x):
    return jnp.maximum(x, 0.0) + jnp.log(1.0 + jnp.exp(-jnp.abs(x)))


def _params(semantics):
    return pltpu.CompilerParams(dimension_semantics=semantics, vmem_limit_bytes=VMEM_LIMIT_BYTES)


def _const_spec(shape):
    zeros = (0,) * len(shape)
    return pl.BlockSpec(shape, lambda *_: zeros)


def _norm_matmul_kernel(x_ref, g_ref, *refs):
    n_out = len(refs) // 2
    w_refs, o_refs = refs[:n_out], refs[n_out:]
    n = _rmsnorm(x_ref[...], g_ref[...]).astype(BF16)
    for w_ref, o_ref in zip(w_refs, o_refs):
        cols = w_ref.shape[1]
        for c0 in range(0, cols, COL_CHUNK):
            c1 = min(c0 + COL_CHUNK, cols)
            o_ref[:, c0:c1] = jnp.dot(n, w_ref[:, c0:c1], preferred_element_type=F32).astype(o_ref.dtype)


def _norm_matmul(x, g, ws, name):
    rows, k = x.shape
    tm = min(TOKEN_TILE, rows)
    return pl.pallas_call(
        _norm_matmul_kernel,
        grid=(rows // tm,),
        in_specs=[pl.BlockSpec((tm, k), lambda i: (i, 0)), _const_spec((1, k))]
        + [_const_spec(w.shape) for w in ws],
        out_specs=[pl.BlockSpec((tm, w.shape[1]), lambda i: (i, 0)) for w in ws],
        out_shape=[jax.ShapeDtypeStruct((rows, w.shape[1]), F32) for w in ws],
        compiler_params=_params(("parallel",)),
        name=name,
    )(x, g, *ws)


def _out_proj_kernel(x_ref, a_ref, b_ref, wa_ref, wb_ref, o_ref):
    acc = jnp.dot(a_ref[...], wa_ref[...], preferred_element_type=F32)
    acc = acc + jnp.dot(b_ref[...], wb_ref[...], preferred_element_type=F32)
    o_ref[...] = x_ref[...] + acc


def _out_proj(x, a, b, wa, wb):
    rows, d = x.shape
    tm = min(TOKEN_TILE, rows)
    row_spec = lambda width: pl.BlockSpec((tm, width), lambda i: (i, 0))
    return pl.pallas_call(
        _out_proj_kernel,
        grid=(rows // tm,),
        in_specs=[row_spec(d), row_spec(a.shape[1]), row_spec(b.shape[1]),
                  _const_spec(wa.shape), _const_spec(wb.shape)],
        out_specs=row_spec(d),
        out_shape=jax.ShapeDtypeStruct((rows, d), F32),
        compiler_params=_params(("parallel",)),
        name="out_proj",
    )(x, a, b, wa, wb)


def _hgrn_kernel(p_ref, s0_ref, lb_ref, gn_ref, o_ref, sn_ref, st_scr, *, chunk):
    c = pl.program_id(1)
    n_blk = chunk // HG_BLOCK
    W = HG_WIDTH

    @pl.when(c == 0)
    def _():
        for h in range(HG_HEADS):
            st_scr[h] = s0_ref[0, h].T

    p = p_ref[0]
    q, fpre, v, gate = p[:, :W], p[:, W:2 * W], p[:, 2 * W:3 * W], p[:, 3 * W:]
    lb = lb_ref[...]
    logf = jnp.log(lb + (1.0 - lb) * jax.nn.sigmoid(fpre))
    k = (1.0 - lb) * jax.nn.sigmoid(-fpre)

    ti = lax.broadcasted_iota(jnp.int32, (chunk, chunk), 0)
    tj = lax.broadcasted_iota(jnp.int32, (chunk, chunk), 1)
    bi, bj = ti // HG_BLOCK, tj // HG_BLOCK
    causal = ti >= tj
    same_blk = bi == bj
    m_incl = causal.astype(F32)
    m_diag = (causal & same_blk).astype(F32)
    m_end = (bj <= bi).astype(F32)
    cums = _fdot(jnp.concatenate([m_incl, m_diag, m_end], axis=0), logf)
    lc, lblk, lend = cums[:chunk], cums[chunk:2 * chunk], cums[2 * chunk:]
    llast = lc[chunk - 1:chunk, :]

    q_in = q * jnp.exp(lc)
    qg = q * jnp.exp(lblk)
    kg = k * jnp.exp(-lblk)
    kd = k * jnp.exp(lend - lc)
    k_end = k * jnp.exp(llast - lc)
    g_chunk = jnp.exp(llast)

    row_blk = lax.broadcasted_iota(jnp.int32, (chunk, 1), 0) // HG_BLOCK
    q_cross, k_cross = [], []
    for j in range(n_blk - 1):
        end_j = lend[j * HG_BLOCK:j * HG_BLOCK + 1, :]
        q_cross.append(jnp.where(row_blk > j, q * jnp.exp(jnp.minimum(lc - end_j, 0.0)), 0.0))
        k_cross.append(jnp.where(row_blk == j, kd, 0.0))

    gn = gn_ref[...]
    for h in range(HG_HEADS):
        sl = slice(h * HG_DIM, (h + 1) * HG_DIM)
        att = jnp.where(causal & same_blk, _bdot(qg[:, sl], kg[:, sl], NT), 0.0)
        for j in range(n_blk - 1):
            att = att + _bdot(q_cross[j][:, sl], k_cross[j][:, sl], NT)
        st = st_scr[h]
        o = _bdot(q_in[:, sl], st, NT) + _bdot(att, v[:, sl])
        st_scr[h] = st * g_chunk[:, sl] + _bdot(v[:, sl], k_end[:, sl], TN)
        o = o * lax.rsqrt(jnp.mean(o * o, axis=-1, keepdims=True) + RMS_EPS)
        o_ref[0, :, sl] = (o * gn[:, sl] * jax.nn.silu(gate[:, sl])).astype(o_ref.dtype)

    @pl.when(c == pl.num_programs(1) - 1)
    def _():
        for h in range(HG_HEADS):
            sn_ref[0, h] = st_scr[h].T


def _hgrn(p, s0, lb, gn):
    bsz, t_len, _ = p.shape
    chunk = min(TIME_CHUNK, t_len)
    return pl.pallas_call(
        functools.partial(_hgrn_kernel, chunk=chunk),
        grid=(bsz, t_len // chunk),
        in_specs=[pl.BlockSpec((1, chunk, HG_COLS), lambda b, c: (b, c, 0)),
                  pl.BlockSpec((1, HG_HEADS, HG_DIM, HG_DIM), lambda b, c: (b, 0, 0, 0)),
                  _const_spec((1, HG_WIDTH)), _const_spec((1, HG_WIDTH))],
        out_specs=[pl.BlockSpec((1, chunk, HG_WIDTH), lambda b, c: (b, c, 0)),
                   pl.BlockSpec((1, HG_HEADS, HG_DIM, HG_DIM), lambda b, c: (b, 0, 0, 0))],
        out_shape=[jax.ShapeDtypeStruct((bsz, t_len, HG_WIDTH), BF16),
                   jax.ShapeDtypeStruct((bsz, HG_HEADS, HG_DIM, HG_DIM), F32)],
        scratch_shapes=[pltpu.VMEM((HG_HEADS, HG_DIM, HG_DIM), F32)],
        compiler_params=_params(("parallel", "arbitrary")),
        name="hgrn",
    )(p, s0, lb, gn)


def _rwkv_kernel(p_ref, s0_ref, sh0_ref, mu_ref, w0_ref, wb_ref, a0_ref, ab_ref, gb_ref, kk_ref, ka_ref,
                 rk_ref, gnw_ref, gnb_ref, j_ref, o_ref, sn_ref, shn_ref, s_scr, sh_scr, *, chunk):
    c = pl.program_id(1)
    W, N, C = RW_WIDTH, RW_HEAD, chunk

    @pl.when(c == 0)
    def _():
        s_scr[...] = s0_ref[0]
        sh_scr[...] = sh0_ref[0]

    p = p_ref[0]
    row = lax.broadcasted_iota(jnp.int32, (C, 1), 0)
    prev = jnp.where(row == 0, sh_scr[...], pltpu.roll(p, 1, 0))
    sh_scr[...] = p[C - 1:C, :]
    xs = p + (prev - p) * mu_ref[...]
    r, k, v = xs[:, :W], xs[:, W:2 * W], xs[:, 2 * W:3 * W]
    wd = xs[:, 3 * W:3 * W + RW_DECAY_LORA]
    ad = xs[:, 3 * W + RW_DECAY_LORA:3 * W + RW_DECAY_LORA + RW_A_LORA]
    gd = xs[:, 3 * W + RW_DECAY_LORA + RW_A_LORA:]

    w_log = -_softplus(-(w0_ref[...] + _bdot(jnp.tanh(wd), wb_ref[...]))) - 0.5
    ld = -jnp.exp(w_log)
    a_icl = jax.nn.sigmoid(a0_ref[...] + _bdot(ad, ab_ref[...]))
    gate = _bdot(jax.nn.sigmoid(gd), gb_ref[...])
    jmat = j_ref[...]
    kk = k * kk_ref[...]
    kk = kk / jnp.maximum(jnp.sqrt(_segsum(kk * kk, jmat)), 1e-12)
    k2 = k * (1.0 + (a_icl - 1.0) * ka_ref[...])
    av = -kk
    bv = kk * a_icl

    ti = lax.broadcasted_iota(jnp.int32, (C, C), 0)
    tj = lax.broadcasted_iota(jnp.int32, (C, C), 1)
    incl = ti >= tj
    strict = ti > tj
    eye = (ti == tj).astype(F32)
    lc = _fdot(incl.astype(F32), ld)
    llast = lc[C - 1:C, :]
    e_in = jnp.exp(lc)
    e_inv = jnp.exp(-lc)
    e_end = jnp.exp(llast - lc)
    a_t = av * jnp.exp(lc - ld)
    r_t = r * e_in
    b_t = bv * e_inv
    k_t = k2 * e_inv
    b_h = bv * e_end
    k_h = k2 * e_end
    g_chunk = jnp.exp(llast)

    ys = []
    for h in range(RW_HEADS):
        sl = slice(h * N, (h + 1) * N)
        g = _bdot(jnp.concatenate([a_t[:, sl], r_t[:, sl]], axis=0),
                  jnp.concatenate([b_t[:, sl], k_t[:, sl]], axis=0), NT)
        l_ab = jnp.where(strict, g[:C, :C], 0.0)
        l_ak = jnp.where(strict, g[:C, C:], 0.0)
        a_rb = jnp.where(incl, g[C:, :C], 0.0)
        a_rk = jnp.where(incl, g[C:, C:], 0.0)
        t_inv = eye + l_ab
        l_pow = l_ab
        span = 2
        while span < C:
            l_pow = _bdot(l_pow, l_pow)
            t_inv = t_inv + _bdot(l_pow, t_inv)
            span *= 2
        vh = v[:, sl]
        au = _bdot(t_inv, jnp.concatenate([a_t[:, sl], _bdot(l_ak, vh)], axis=1))
        mn = _bdot(au, b_h[:, sl], TN)
        nt = mn[N:] + _bdot(vh, k_h[:, sl], TN)
        ry = _bdot(a_rb, au)
        r_hat = r_t[:, sl] + ry[:, :N]
        y0 = ry[:, N:] + _bdot(a_rk, vh)
        s = s_scr[h]
        ys.append(_bdot(r_hat, s, NT) + y0)
        s_scr[h] = s * g_chunk[:, sl] + _bdot(s, mn[:N]) + nt

    y = jnp.concatenate(ys, axis=1)
    mean = _segsum(y, jmat) * (1.0 / N)
    yc = y - mean
    var = _segsum(yc * yc, jmat) * (1.0 / N)
    yn = yc * lax.rsqrt(var + GN_EPS) * gnw_ref[...] + gnb_ref[...]
    bonus = _segsum(r * k2 * rk_ref[...], jmat) * v
    o_ref[0] = ((yn + bonus) * gate).astype(o_ref.dtype)

    @pl.when(c == pl.num_programs(1) - 1)
    def _():
        sn_ref[0] = s_scr[...]
        shn_ref[0] = sh_scr[...]


def _rwkv(p, s0, sh0, vecs, wb, ab, gb, jmat):
    bsz, t_len, _ = p.shape
    chunk = min(TIME_CHUNK, t_len)
    mu, w0, a0, kk, ka, rk, gnw, gnb = vecs
    state_spec = pl.BlockSpec((1, RW_HEADS, RW_HEAD, RW_HEAD), lambda b, c: (b, 0, 0, 0))
    shift_spec = pl.BlockSpec((1, 1, RW_COLS), lambda b, c: (b, 0, 0))
    return pl.pallas_call(
        functools.partial(_rwkv_kernel, chunk=chunk),
        grid=(bsz, t_len // chunk),
        in_specs=[pl.BlockSpec((1, chunk, RW_COLS), lambda b, c: (b, c, 0)), state_spec, shift_spec,
                  _const_spec(mu.shape), _const_spec(w0.shape), _const_spec(wb.shape), _const_spec(a0.shape),
                  _const_spec(ab.shape), _const_spec(gb.shape), _const_spec(kk.shape), _const_spec(ka.shape),
                  _const_spec(rk.shape), _const_spec(gnw.shape), _const_spec(gnb.shape), _const_spec(jmat.shape)],
        out_specs=[pl.BlockSpec((1, chunk, RW_WIDTH), lambda b, c: (b, c, 0)), state_spec, shift_spec],
        out_shape=[jax.ShapeDtypeStruct((bsz, t_len, RW_WIDTH), BF16),
                   jax.ShapeDtypeStruct((bsz, RW_HEADS, RW_HEAD, RW_HEAD), F32),
                   jax.ShapeDtypeStruct((bsz, 1, RW_COLS), F32)],
        scratch_shapes=[pltpu.VMEM((RW_HEADS, RW_HEAD, RW_HEAD), F32), pltpu.VMEM((1, RW_COLS), F32)],
        compiler_params=_params(("parallel", "arbitrary")),
        name="rwkv",
    )(p, s0, sh0, mu, w0, wb, a0, ab, gb, kk, ka, rk, gnw, gnb, jmat)


def _cross_kernel(x_ref, k_ref, v_ref, g_ref, wq_ref, wo_ref, o_ref):
    x = x_ref[0]
    q = jnp.dot(_rmsnorm(x, g_ref[...]).astype(BF16), wq_ref[...], preferred_element_type=F32)
    heads = []
    for h in range(X_HEADS):
        sl = slice(h * X_DIM, (h + 1) * X_DIM)
        s = _bdot(q[:, sl], k_ref[0, :, sl], NT) * (X_DIM ** -0.5)
        e = jnp.exp(s - jnp.max(s, axis=-1, keepdims=True))
        pr = e / jnp.sum(e, axis=-1, keepdims=True)
        heads.append(_bdot(pr, v_ref[0, :, sl]).astype(BF16))
    o_ref[0] = x + jnp.dot(jnp.concatenate(heads, axis=1), wo_ref[...], preferred_element_type=F32)


def _cross(x, mk, mv, g, wq, wo):
    bsz, t_len, d = x.shape
    tq = min(TOKEN_TILE, t_len)
    x_spec = pl.BlockSpec((1, tq, d), lambda b, t: (b, t, 0))
    mem_spec = pl.BlockSpec((1, N_MEM, d), lambda b, t: (b, 0, 0))
    return pl.pallas_call(
        _cross_kernel,
        grid=(bsz, t_len // tq),
        in_specs=[x_spec, mem_spec, mem_spec, _const_spec((1, d)), _const_spec(wq.shape), _const_spec(wo.shape)],
        out_specs=x_spec,
        out_shape=jax.ShapeDtypeStruct(x.shape, F32),
        compiler_params=_params(("parallel", "parallel")),
        name="cross",
    )(x, mk, mv, g, wq, wo)


def _ffn_kernel(x_ref, g_ref, w1_ref, w3_ref, w2_ref, gf_ref, o_ref):
    x = x_ref[...]
    n = _rmsnorm(x, g_ref[...]).astype(BF16)
    acc = x
    for c0 in range(0, D_FF, FF_CHUNK):
        c1 = c0 + FF_CHUNK
        up = jnp.dot(n, w1_ref[:, c0:c1], preferred_element_type=F32)
        lin = jnp.dot(n, w3_ref[:, c0:c1], preferred_element_type=F32)
        hid = (jax.nn.silu(up) * lin).astype(BF16)
        acc = acc + jnp.dot(hid, w2_ref[c0:c1, :], preferred_element_type=F32)
    o_ref[...] = _rmsnorm(acc, gf_ref[...])


def _ffn(x, g, w1, w3, w2, gf):
    rows, d = x.shape
    tm = min(TOKEN_TILE, rows)
    row_spec = pl.BlockSpec((tm, d), lambda i: (i, 0))
    return pl.pallas_call(
        _ffn_kernel,
        grid=(rows // tm,),
        in_specs=[row_spec, _const_spec((1, d)), _const_spec(w1.shape), _const_spec(w3.shape),
                  _const_spec(w2.shape), _const_spec((1, d))],
        out_specs=row_spec,
        out_shape=jax.ShapeDtypeStruct((rows, d), F32),
        compiler_params=_params(("parallel",)),
        name="ffn",
    )(x, g, w1, w3, w2, gf)


def _trunk(x, mk, mv, hg0, rw0, sh0, lb, w):
    bsz, t_len, d = x.shape
    x2d = x.reshape(bsz * t_len, d)
    p_hg, p_rw = _norm_matmul(x2d, w['norm_mix'], [w['w_in_hg'], w['w_in_rw']], "in_proj")
    hg_o, hg_s = _hgrn(p_hg.reshape(bsz, t_len, HG_COLS), hg0, lb, w['hgrn_norm'])
    rw_o, rw_s, rw_sh = _rwkv(p_rw.reshape(bsz, t_len, RW_COLS), rw0, sh0, w['rw_vecs'],
                              w['rw_w_b'], w['rw_a_b'], w['rw_g_b'], w['jmat'])
    x2d = _out_proj(x2d, hg_o.reshape(bsz * t_len, HG_WIDTH), rw_o.reshape(bsz * t_len, RW_WIDTH),
                    w['w_out_hg'], w['w_out_rw'])
    x3d = _cross(x2d.reshape(bsz, t_len, d), mk, mv, w['norm_cross'], w['w_cq'], w['w_co'])
    y = _ffn(x3d.reshape(bsz * t_len, d), w['norm_ffn'], w['w_ff1'], w['w_ff3'], w['w_ff2'], w['norm_final'])
    return y.reshape(bsz, t_len, d), hg_s, rw_s, rw_sh


def kernel(x_prompt, mem_prompt, x_sample, cache_mem_k, cache_mem_v, state_hgrn, state_rwkv, state_rwkv_shift, hgrn_lb_logits, norm_mix, w_in, hgrn_norm, rw_mu, rw_w0, rw_w_b, rw_a0, rw_a_b, rw_g_b, rw_k_k, rw_k_a, rw_r_k, rw_gn_w, rw_gn_b, w_out, norm_cross, norm_mem, w_cq, w_ck, w_cv, w_co, norm_ffn, w_ff1, w_ff3, w_ff2, norm_final):
    assert w_in.shape[0] == 1, "single-layer trunk"
    bp = x_prompt.shape[0]
    row = lambda a: a.reshape(1, -1).astype(F32)
    bf = lambda a: a.astype(BF16)
    lb = jnp.cumsum(jax.nn.softmax(hgrn_lb_logits.astype(F32), axis=0), axis=0)[0:1]
    head_id = jnp.arange(RW_WIDTH, dtype=jnp.int32) // RW_HEAD
    w = {
        'norm_mix': row(norm_mix[0]), 'w_in_hg': bf(w_in[0, :, :HG_COLS]), 'w_in_rw': bf(w_in[0, :, HG_COLS:]),
        'hgrn_norm': row(hgrn_norm[0]),
        'rw_vecs': tuple(row(a[0]) for a in (rw_mu, rw_w0, rw_a0, rw_k_k, rw_k_a, rw_r_k, rw_gn_w, rw_gn_b)),
        'rw_w_b': bf(rw_w_b[0]), 'rw_a_b': bf(rw_a_b[0]), 'rw_g_b': bf(rw_g_b[0]),
        'jmat': (head_id[:, None] == head_id[None, :]).astype(BF16),
        'w_out_hg': bf(w_out[0, :HG_WIDTH]), 'w_out_rw': bf(w_out[0, HG_WIDTH:]),
        'norm_cross': row(norm_cross[0]), 'w_cq': bf(w_cq[0]), 'w_co': bf(w_co[0]),
        'norm_ffn': row(norm_ffn[0]), 'w_ff1': bf(w_ff1[0]), 'w_ff3': bf(w_ff3[0]), 'w_ff2': bf(w_ff2[0]),
        'norm_final': row(norm_final),
    }
    mk, mv = _norm_matmul(mem_prompt.reshape(bp * N_MEM, D_MODEL), row(norm_mem[0]),
                          [bf(w_ck[0]), bf(w_cv[0])], "mem_kv")
    mk = mk.reshape(bp, N_MEM, D_MODEL)
    mv = mv.reshape(bp, N_MEM, D_MODEL)
    y_p, p_hg, p_rw, p_sh = _trunk(
        x_prompt, mk, mv, jnp.zeros((bp, HG_HEADS, HG_DIM, HG_DIM), F32),
        jnp.zeros((bp, RW_HEADS, RW_HEAD, RW_HEAD), F32), jnp.zeros((bp, 1, RW_COLS), F32), lb, w)
    bs = x_sample.shape[0]
    y_s, s_hg, s_rw, s_sh = _trunk(
        x_sample, cache_mem_k[0].reshape(bs, N_MEM, D_MODEL), cache_mem_v[0].reshape(bs, N_MEM, D_MODEL),
        state_hgrn[0], state_rwkv[0], state_rwkv_shift[0], lb, w)
    return (y_p, y_s, p_hg[None], p_rw[None], p_sh[None],
            mk.reshape(1, bp, N_MEM, X_HEADS, X_DIM), mv.reshape(1, bp, N_MEM, X_HEADS, X_DIM),
            s_hg[None], s_rw[None], s_sh[None])
```

```python
import functools

import jax
import jax.numpy as jnp
from jax import lax
from jax.experimental import pallas as pl
from jax.experimental.pallas import tpu as pltpu

F32 = jnp.float32
BF16 = jnp.bfloat16

D_MODEL = 1024
HG_WIDTH = 512
HG_HEADS = 4
HG_DIM = 128
HG_BLOCK = 16
HG_COLS = 4 * HG_WIDTH
RW_WIDTH = 512
RW_HEAD = 64
RW_HEADS = 8
RW_DECAY_LORA = 64
RW_A_LORA = 64
RW_GATE_LORA = 128
RW_COLS = 3 * RW_WIDTH + RW_DECAY_LORA + RW_A_LORA + RW_GATE_LORA
N_MEM = 256
X_HEADS = 4
X_DIM = D_MODEL // X_HEADS
D_FF = 2816
RMS_EPS = 1e-6
GN_EPS = 64e-5

TIME_CHUNK = 64
TOKEN_TILE = 512
FF_CHUNK = 1408
COL_CHUNK = 1024
VMEM_LIMIT_BYTES = 56 * 1024 * 1024

NN = (((1,), (0,)), ((), ()))
NT = (((1,), (1,)), ((), ()))
TN = (((0,), (0,)), ((), ()))


def _bdot(a, b, dims=NN):
    return lax.dot_general(a.astype(BF16), b.astype(BF16), dims, preferred_element_type=F32)


def _fdot(a, b, dims=NN):
    return lax.dot_general(a, b, dims, precision=lax.Precision.HIGHEST, preferred_element_type=F32)


def _segsum(x, j_bf16):
    rows = x.shape[0]
    hi = x.astype(BF16)
    lo = (x - hi.astype(F32)).astype(BF16)
    both = jnp.dot(jnp.concatenate([hi, lo], axis=0), j_bf16, preferred_element_type=F32)
    return both[:rows] + both[rows:]


def _rmsnorm(x, g):
    return x * lax.rsqrt(jnp.mean(x * x, axis=-1, keepdims=True) + RMS_EPS) * g


def _softplus(x):
    return jnp.maximum(x, 0.0) + jnp.log(1.0 + jnp.exp(-jnp.abs(x)))


def _params(semantics):
    return pltpu.CompilerParams(dimension_semantics=semantics, vmem_limit_bytes=VMEM_LIMIT_BYTES)


def _const_spec(shape):
    zeros = (0,) * len(shape)
    return pl.BlockSpec(shape, lambda *_: zeros)


def _norm_matmul_kernel(x_ref, g_ref, *refs):
    n_out = len(refs) // 2
    w_refs, o_refs = refs[:n_out], refs[n_out:]
    n = _rmsnorm(x_ref[...], g_ref[...]).astype(BF16)
    for w_ref, o_ref in zip(w_refs, o_refs):
        cols = w_ref.shape[1]
        for c0 in range(0, cols, COL_CHUNK):
            c1 = min(c0 + COL_CHUNK, cols)
            o_ref[:, c0:c1] = jnp.dot(n, w_ref[:, c0:c1], preferred_element_type=F32).astype(o_ref.dtype)


def _norm_matmul(x, g, ws, name):
    rows, k = x.shape
    tm = min(TOKEN_TILE, rows)
    return pl.pallas_call(
        _norm_matmul_kernel,
        grid=(rows // tm,),
        in_specs=[pl.BlockSpec((tm, k), lambda i: (i, 0)), _const_spec((1, k))]
        + [_const_spec(w.shape) for w in ws],
        out_specs=[pl.BlockSpec((tm, w.shape[1]), lambda i: (i, 0)) for w in ws],
        out_shape=[jax.ShapeDtypeStruct((rows, w.shape[1]), F32) for w in ws],
        compiler_params=_params(("parallel",)),
        name=name,
    )(x, g, *ws)


def _out_proj_kernel(x_ref, a_ref, b_ref, wa_ref, wb_ref, o_ref):
    acc = jnp.dot(a_ref[...], wa_ref[...], preferred_element_type=F32)
    acc = acc + jnp.dot(b_ref[...], wb_ref[...], preferred_element_type=F32)
    o_ref[...] = x_ref[...] + acc


def _out_proj(x, a, b, wa, wb):
    rows, d = x.shape
    tm = min(TOKEN_TILE, rows)
    row_spec = lambda width: pl.BlockSpec((tm, width), lambda i: (i, 0))
    return pl.pallas_call(
        _out_proj_kernel,
        grid=(rows // tm,),
        in_specs=[row_spec(d), row_spec(a.shape[1]), row_spec(b.shape[1]),
                  _const_spec(wa.shape), _const_spec(wb.shape)],
        out_specs=row_spec(d),
        out_shape=jax.ShapeDtypeStruct((rows, d), F32),
        compiler_params=_params(("parallel",)),
        name="out_proj",
    )(x, a, b, wa, wb)


def _hgrn_kernel(p_ref, s0_ref, lb_ref, gn_ref, o_ref, sn_ref, st_scr, *, chunk):
    c = pl.program_id(1)
    n_blk = chunk // HG_BLOCK
    W = HG_WIDTH

    @pl.when(c == 0)
    def _():
        for h in range(HG_HEADS):
            st_scr[h] = s0_ref[0, h].T

    p = p_ref[0]
    q, fpre, v, gate = p[:, :W], p[:, W:2 * W], p[:, 2 * W:3 * W], p[:, 3 * W:]
    lb = lb_ref[...]
    logf = jnp.log(lb + (1.0 - lb) * jax.nn.sigmoid(fpre))
    k = (1.0 - lb) * jax.nn.sigmoid(-fpre)

    ti = lax.broadcasted_iota(jnp.int32, (chunk, chunk), 0)
    tj = lax.broadcasted_iota(jnp.int32, (chunk, chunk), 1)
    bi, bj = ti // HG_BLOCK, tj // HG_BLOCK
    causal = ti >= tj
    same_blk = bi == bj
    m_incl = causal.astype(F32)
    m_diag = (causal & same_blk).astype(F32)
    m_end = (bj <= bi).astype(F32)
    cums = _fdot(jnp.concatenate([m_incl, m_diag, m_end], axis=0), logf)
    lc, lblk, lend = cums[:chunk], cums[chunk:2 * chunk], cums[2 * chunk:]
    llast = lc[chunk - 1:chunk, :]

    q_in = q * jnp.exp(lc)
    qg = q * jnp.exp(lblk)
    kg = k * jnp.exp(-lblk)
    kd = k * jnp.exp(lend - lc)
    k_end = k * jnp.exp(llast - lc)
    g_chunk = jnp.exp(llast)

    row_blk = lax.broadcasted_iota(jnp.int32, (chunk, 1), 0) // HG_BLOCK
    q_cross, k_cross = [], []
    for j in range(n_blk - 1):
        end_j = lend[j * HG_BLOCK:j * HG_BLOCK + 1, :]
        q_cross.append(jnp.where(row_blk > j, q * jnp.exp(jnp.minimum(lc - end_j, 0.0)), 0.0))
        k_cross.append(jnp.where(row_blk == j, kd, 0.0))

    gn = gn_ref[...]
    for h in range(HG_HEADS):
        sl = slice(h * HG_DIM, (h + 1) * HG_DIM)
        att = jnp.where(causal & same_blk, _bdot(qg[:, sl], kg[:, sl], NT), 0.0)
        for j in range(n_blk - 1):
            att = att + _bdot(q_cross[j][:, sl], k_cross[j][:, sl], NT)
        st = st_scr[h]
        o = _bdot(q_in[:, sl], st, NT) + _bdot(att, v[:, sl])
        st_scr[h] = st * g_chunk[:, sl] + _bdot(v[:, sl], k_end[:, sl], TN)
        o = o * lax.rsqrt(jnp.mean(o * o, axis=-1, keepdims=True) + RMS_EPS)
        o_ref[0, :, sl] = (o * gn[:, sl] * jax.nn.silu(gate[:, sl])).astype(o_ref.dtype)

    @pl.when(c == pl.num_programs(1) - 1)
    def _():
        for h in range(HG_HEADS):
            sn_ref[0, h] = st_scr[h].T


def _hgrn(p, s0, lb, gn):
    bsz, t_len, _ = p.shape
    chunk = min(TIME_CHUNK, t_len)
    return pl.pallas_call(
        functools.partial(_hgrn_kernel, chunk=chunk),
        grid=(bsz, t_len // chunk),
        in_specs=[pl.BlockSpec((1, chunk, HG_COLS), lambda b, c: (b, c, 0)),
                  pl.BlockSpec((1, HG_HEADS, HG_DIM, HG_DIM), lambda b, c: (b, 0, 0, 0)),
                  _const_spec((1, HG_WIDTH)), _const_spec((1, HG_WIDTH))],
        out_specs=[pl.BlockSpec((1, chunk, HG_WIDTH), lambda b, c: (b, c, 0)),
                   pl.BlockSpec((1, HG_HEADS, HG_DIM, HG_DIM), lambda b, c: (b, 0, 0, 0))],
        out_shape=[jax.ShapeDtypeStruct((bsz, t_len, HG_WIDTH), BF16),
                   jax.ShapeDtypeStruct((bsz, HG_HEADS, HG_DIM, HG_DIM), F32)],
        scratch_shapes=[pltpu.VMEM((HG_HEADS, HG_DIM, HG_DIM), F32)],
        compiler_params=_params(("parallel", "arbitrary")),
        name="hgrn",
    )(p, s0, lb, gn)


def _rwkv_kernel(p_ref, s0_ref, sh0_ref, mu_ref, w0_ref, wb_ref, a0_ref, ab_ref, gb_ref, kk_ref, ka_ref,
                 rk_ref, gnw_ref, gnb_ref, j_ref, o_ref, sn_ref, shn_ref, s_scr, sh_scr, *, chunk):
    c = pl.program_id(1)
    W, N, C = RW_WIDTH, RW_HEAD, chunk

    @pl.when(c == 0)
    def _():
        s_scr[...] = s0_ref[0]
        sh_scr[...] = sh0_ref[0]

    p = p_ref[0]
    row = lax.broadcasted_iota(jnp.int32, (C, 1), 0)
    prev = jnp.where(row == 0, sh_scr[...], pltpu.roll(p, 1, 0))
    sh_scr[...] = p[C - 1:C, :]
    xs = p + (prev - p) * mu_ref[...]
    r, k, v = xs[:, :W], xs[:, W:2 * W], xs[:, 2 * W:3 * W]
    wd = xs[:, 3 * W:3 * W + RW_DECAY_LORA]
    ad = xs[:, 3 * W + RW_DECAY_LORA:3 * W + RW_DECAY_LORA + RW_A_LORA]
    gd = xs[:, 3 * W + RW_DECAY_LORA + RW_A_LORA:]

    w_log = -_softplus(-(w0_ref[...] + _bdot(jnp.tanh(wd), wb_ref[...]))) - 0.5
    ld = -jnp.exp(w_log)
    a_icl = jax.nn.sigmoid(a0_ref[...] + _bdot(ad, ab_ref[...]))
    gate = _bdot(jax.nn.sigmoid(gd), gb_ref[...])
    jmat = j_ref[...]
    kk = k * kk_ref[...]
    kk = kk / jnp.maximum(jnp.sqrt(_segsum(kk * kk, jmat)), 1e-12)
    k2 = k * (1.0 + (a_icl - 1.0) * ka_ref[...])
    av = -kk
    bv = kk * a_icl

    ti = lax.broadcasted_iota(jnp.int32, (C, C), 0)
    tj = lax.broadcasted_iota(jnp.int32, (C, C), 1)
    incl = ti >= tj
    strict = ti > tj
    lc = _fdot(incl.astype(F32), ld)
    llast = lc[C - 1:C, :]
    e_in = jnp.exp(lc)
    e_inv = jnp.exp(-lc)
    e_end = jnp.exp(llast - lc)
    a_t = av * jnp.exp(lc - ld)
    r_t = r * e_in
    b_t = bv * e_inv
    k_t = k2 * e_inv
    b_h = bv * e_end
    k_h = k2 * e_end
    g_chunk = jnp.exp(llast)

    heads = [slice(h * N, (h + 1) * N) for h in range(RW_HEADS)]
    hr = range(RW_HEADS)
    g = [_bdot(jnp.concatenate([a_t[:, sl], r_t[:, sl]], axis=0),
               jnp.concatenate([b_t[:, sl], k_t[:, sl]], axis=0), NT) for sl in heads]
    l_pow = [jnp.where(strict, g[h][:C, :C], 0.0) for h in hr]
    l_ak = [jnp.where(strict, g[h][:C, C:], 0.0) for h in hr]
    a_rb = [jnp.where(incl, g[h][C:, :C], 0.0) for h in hr]
    a_rk = [jnp.where(incl, g[h][C:, C:], 0.0) for h in hr]
    vh = [v[:, sl] for sl in heads]
    au = [jnp.concatenate([a_t[:, heads[h]], _bdot(l_ak[h], vh[h])], axis=1) for h in hr]
    span = 1
    while True:
        au = [au[h] + _bdot(l_pow[h], au[h]) for h in hr]
        span *= 2
        if span >= C:
            break
        l_pow = [_bdot(l, l) for l in l_pow]
    mn = [_bdot(au[h], b_h[:, heads[h]], TN) for h in hr]
    nt = [mn[h][N:] + _bdot(vh[h], k_h[:, heads[h]], TN) for h in hr]
    ry = [_bdot(a_rb[h], au[h]) for h in hr]
    y0 = [ry[h][:, N:] + _bdot(a_rk[h], vh[h]) for h in hr]
    s_old = [s_scr[h] for h in hr]
    ys = [_bdot(r_t[:, heads[h]] + ry[h][:, :N], s_old[h], NT) + y0[h] for h in hr]
    for h in hr:
        s_scr[h] = s_old[h] * g_chunk[:, heads[h]] + _bdot(s_old[h], mn[h][:N]) + nt[h]

    y = jnp.concatenate(ys, axis=1)
    mean = _segsum(y, jmat) * (1.0 / N)
    yc = y - mean
    var = _segsum(yc * yc, jmat) * (1.0 / N)
    yn = yc * lax.rsqrt(var + GN_EPS) * gnw_ref[...] + gnb_ref[...]
    bonus = _segsum(r * k2 * rk_ref[...], jmat) * v
    o_ref[0] = ((yn + bonus) * gate).astype(o_ref.dtype)

    @pl.when(c == pl.num_programs(1) - 1)
    def _():
        sn_ref[0] = s_scr[...]
        shn_ref[0] = sh_scr[...]


def _rwkv(p, s0, sh0, vecs, wb, ab, gb, jmat):
    bsz, t_len, _ = p.shape
    chunk = min(TIME_CHUNK, t_len)
    mu, w0, a0, kk, ka, rk, gnw, gnb = vecs
    state_spec = pl.BlockSpec((1, RW_HEADS, RW_HEAD, RW_HEAD), lambda b, c: (b, 0, 0, 0))
    shift_spec = pl.BlockSpec((1, 1, RW_COLS), lambda b, c: (b, 0, 0))
    return pl.pallas_call(
        functools.partial(_rwkv_kernel, chunk=chunk),
        grid=(bsz, t_len // chunk),
        in_specs=[pl.BlockSpec((1, chunk, RW_COLS), lambda b, c: (b, c, 0)), state_spec, shift_spec,
                  _const_spec(mu.shape), _const_spec(w0.shape), _const_spec(wb.shape), _const_spec(a0.shape),
                  _const_spec(ab.shape), _const_spec(gb.shape), _const_spec(kk.shape), _const_spec(ka.shape),
                  _const_spec(rk.shape), _const_spec(gnw.shape), _const_spec(gnb.shape), _const_spec(jmat.shape)],
        out_specs=[pl.BlockSpec((1, chunk, RW_WIDTH), lambda b, c: (b, c, 0)), state_spec, shift_spec],
        out_shape=[jax.ShapeDtypeStruct((bsz, t_len, RW_WIDTH), BF16),
                   jax.ShapeDtypeStruct((bsz, RW_HEADS, RW_HEAD, RW_HEAD), F32),
                   jax.ShapeDtypeStruct((bsz, 1, RW_COLS), F32)],
        scratch_shapes=[pltpu.VMEM((RW_HEADS, RW_HEAD, RW_HEAD), F32), pltpu.VMEM((1, RW_COLS), F32)],
        compiler_params=_params(("parallel", "arbitrary")),
        name="rwkv",
    )(p, s0, sh0, mu, w0, wb, a0, ab, gb, kk, ka, rk, gnw, gnb, jmat)


def _cross_kernel(x_ref, k_ref, v_ref, g_ref, wq_ref, wo_ref, o_ref):
    x = x_ref[0]
    q = jnp.dot(_rmsnorm(x, g_ref[...]).astype(BF16), wq_ref[...], preferred_element_type=F32)
    heads = []
    for h in range(X_HEADS):
        sl = slice(h * X_DIM, (h + 1) * X_DIM)
        s = _bdot(q[:, sl], k_ref[0, :, sl], NT) * (X_DIM ** -0.5)
        e = jnp.exp(s - jnp.max(s, axis=-1, keepdims=True))
        pr = e / jnp.sum(e, axis=-1, keepdims=True)
        heads.append(_bdot(pr, v_ref[0, :, sl]).astype(BF16))
    o_ref[0] = x + jnp.dot(jnp.concatenate(heads, axis=1), wo_ref[...], preferred_element_type=F32)


def _cross(x, mk, mv, g, wq, wo):
    bsz, t_len, d = x.shape
    tq = min(TOKEN_TILE, t_len)
    x_spec = pl.BlockSpec((1, tq, d), lambda b, t: (b, t, 0))
    mem_spec = pl.BlockSpec((1, N_MEM, d), lambda b, t: (b, 0, 0))
    return pl.pallas_call(
        _cross_kernel,
        grid=(bsz, t_len // tq),
        in_specs=[x_spec, mem_spec, mem_spec, _const_spec((1, d)), _const_spec(wq.shape), _const_spec(wo.shape)],
        out_specs=x_spec,
        out_shape=jax.ShapeDtypeStruct(x.shape, F32),
        compiler_params=_params(("parallel", "parallel")),
        name="cross",
    )(x, mk, mv, g, wq, wo)


def _ffn_kernel(x_ref, g_ref, w1_ref, w3_ref, w2_ref, gf_ref, o_ref):
    x = x_ref[...]
    n = _rmsnorm(x, g_ref[...]).astype(BF16)
    acc = x
    for c0 in range(0, D_FF, FF_CHUNK):
        c1 = c0 + FF_CHUNK
        up = jnp.dot(n, w1_ref[:, c0:c1], preferred_element_type=F32)
        lin = jnp.dot(n, w3_ref[:, c0:c1], preferred_element_type=F32)
        hid = (jax.nn.silu(up) * lin).astype(BF16)
        acc = acc + jnp.dot(hid, w2_ref[c0:c1, :], preferred_element_type=F32)
    o_ref[...] = _rmsnorm(acc, gf_ref[...])


def _ffn(x, g, w1, w3, w2, gf):
    rows, d = x.shape
    tm = min(TOKEN_TILE, rows)
    row_spec = pl.BlockSpec((tm, d), lambda i: (i, 0))
    return pl.pallas_call(
        _ffn_kernel,
        grid=(rows // tm,),
        in_specs=[row_spec, _const_spec((1, d)), _const_spec(w1.shape), _const_spec(w3.shape),
                  _const_spec(w2.shape), _const_spec((1, d))],
        out_specs=row_spec,
        out_shape=jax.ShapeDtypeStruct((rows, d), F32),
        compiler_params=_params(("parallel",)),
        name="ffn",
    )(x, g, w1, w3, w2, gf)


def _trunk(x, mk, mv, hg0, rw0, sh0, lb, w):
    bsz, t_len, d = x.shape
    x2d = x.reshape(bsz * t_len, d)
    p_hg, p_rw = _norm_matmul(x2d, w['norm_mix'], [w['w_in_hg'], w['w_in_rw']], "in_proj")
    hg_o, hg_s = _hgrn(p_hg.reshape(bsz, t_len, HG_COLS), hg0, lb, w['hgrn_norm'])
    rw_o, rw_s, rw_sh = _rwkv(p_rw.reshape(bsz, t_len, RW_COLS), rw0, sh0, w['rw_vecs'],
                              w['rw_w_b'], w['rw_a_b'], w['rw_g_b'], w['jmat'])
    x2d = _out_proj(x2d, hg_o.reshape(bsz * t_len, HG_WIDTH), rw_o.reshape(bsz * t_len, RW_WIDTH),
                    w['w_out_hg'], w['w_out_rw'])
    x3d = _cross(x2d.reshape(bsz, t_len, d), mk, mv, w['norm_cross'], w['w_cq'], w['w_co'])
    y = _ffn(x3d.reshape(bsz * t_len, d), w['norm_ffn'], w['w_ff1'], w['w_ff3'], w['w_ff2'], w['norm_final'])
    return y.reshape(bsz, t_len, d), hg_s, rw_s, rw_sh


def kernel(x_prompt, mem_prompt, x_sample, cache_mem_k, cache_mem_v, state_hgrn, state_rwkv, state_rwkv_shift, hgrn_lb_logits, norm_mix, w_in, hgrn_norm, rw_mu, rw_w0, rw_w_b, rw_a0, rw_a_b, rw_g_b, rw_k_k, rw_k_a, rw_r_k, rw_gn_w, rw_gn_b, w_out, norm_cross, norm_mem, w_cq, w_ck, w_cv, w_co, norm_ffn, w_ff1, w_ff3, w_ff2, norm_final):
    assert w_in.shape[0] == 1, "single-layer trunk"
    bp = x_prompt.shape[0]
    row = lambda a: a.reshape(1, -1).astype(F32)
    bf = lambda a: a.astype(BF16)
    lb = jnp.cumsum(jax.nn.softmax(hgrn_lb_logits.astype(F32), axis=0), axis=0)[0:1]
    head_id = jnp.arange(RW_WIDTH, dtype=jnp.int32) // RW_HEAD
    w = {
        'norm_mix': row(norm_mix[0]), 'w_in_hg': bf(w_in[0, :, :HG_COLS]), 'w_in_rw': bf(w_in[0, :, HG_COLS:]),
        'hgrn_norm': row(hgrn_norm[0]),
        'rw_vecs': tuple(row(a[0]) for a in (rw_mu, rw_w0, rw_a0, rw_k_k, rw_k_a, rw_r_k, rw_gn_w, rw_gn_b)),
        'rw_w_b': bf(rw_w_b[0]), 'rw_a_b': bf(rw_a_b[0]), 'rw_g_b': bf(rw_g_b[0]),
        'jmat': (head_id[:, None] == head_id[None, :]).astype(BF16),
        'w_out_hg': bf(w_out[0, :HG_WIDTH]), 'w_out_rw': bf(w_out[0, HG_WIDTH:]),
        'norm_cross': row(norm_cross[0]), 'w_cq': bf(w_cq[0]), 'w_co': bf(w_co[0]),
        'norm_ffn': row(norm_ffn[0]), 'w_ff1': bf(w_ff1[0]), 'w_ff3': bf(w_ff3[0]), 'w_ff2': bf(w_ff2[0]),
        'norm_final': row(norm_final),
    }
    mk, mv = _norm_matmul(mem_prompt.reshape(bp * N_MEM, D_MODEL), row(norm_mem[0]),
                          [bf(w_ck[0]), bf(w_cv[0])], "mem_kv")
    mk = mk.reshape(bp, N_MEM, D_MODEL)
    mv = mv.reshape(bp, N_MEM, D_MODEL)
    y_p, p_hg, p_rw, p_sh = _trunk(
        x_prompt, mk, mv, jnp.zeros((bp, HG_HEADS, HG_DIM, HG_DIM), F32),
        jnp.zeros((bp, RW_HEADS, RW_HEAD, RW_HEAD), F32), jnp.zeros((bp, 1, RW_COLS), F32), lb, w)
    bs = x_sample.shape[0]
    y_s, s_hg, s_rw, s_sh = _trunk(
        x_sample, cache_mem_k[0].reshape(bs, N_MEM, D_MODEL), cache_mem_v[0].reshape(bs, N_MEM, D_MODEL),
        state_hgrn[0], state_rwkv[0], state_rwkv_shift[0], lb, w)
    return (y_p, y_s, p_hg[None], p_rw[None], p_sh[None],
            mk.reshape(1, bp, N_MEM, X_HEADS, X_DIM), mv.reshape(1, bp, N_MEM, X_HEADS, X_DIM),
            s_hg[None], s_rw[None], s_sh[None])
```

```python
import functools

import jax
import jax.numpy as jnp
from jax import lax
from jax.experimental import pallas as pl
from jax.experimental.pallas import tpu as pltpu

F32 = jnp.float32
BF16 = jnp.bfloat16

D_MODEL = 1024
HG_WIDTH = 512
HG_HEADS = 4
HG_DIM = 128
HG_BLOCK = 16
HG_COLS = 4 * HG_WIDTH
RW_WIDTH = 512
RW_HEAD = 64
RW_HEADS = 8
RW_DECAY_LORA = 64
RW_A_LORA = 64
RW_GATE_LORA = 128
RW_COLS = 3 * RW_WIDTH + RW_DECAY_LORA + RW_A_LORA + RW_GATE_LORA
N_MEM = 256
X_HEADS = 4
X_DIM = D_MODEL // X_HEADS
D_FF = 2816
RMS_EPS = 1e-6
GN_EPS = 64e-5

TIME_CHUNK = 64
TIME_BLOCK = 256
TOKEN_TILE = 512
FF_CHUNK = 1408
COL_CHUNK = 1024
VMEM_LIMIT_BYTES = 56 * 1024 * 1024

NN = (((1,), (0,)), ((), ()))
NT = (((1,), (1,)), ((), ()))
TN = (((0,), (0,)), ((), ()))


def _bdot(a, b, dims=NN):
    return lax.dot_general(a.astype(BF16), b.astype(BF16), dims, preferred_element_type=F32)


def _split3(x):
    hi = x.astype(BF16)
    r1 = x - hi.astype(F32)
    mid = r1.astype(BF16)
    lo = (r1 - mid.astype(F32)).astype(BF16)
    return hi, mid, lo


def _cumdot(mask3, x):
    return jnp.dot(mask3, jnp.concatenate(_split3(x), axis=0), preferred_element_type=F32)


def _segsum(x, j_bf16):
    rows = x.shape[0]
    hi = x.astype(BF16)
    lo = (x - hi.astype(F32)).astype(BF16)
    both = jnp.dot(jnp.concatenate([hi, lo], axis=0), j_bf16, preferred_element_type=F32)
    return both[:rows] + both[rows:]


def _rmsnorm(x, g):
    return x * lax.rsqrt(jnp.mean(x * x, axis=-1, keepdims=True) + RMS_EPS) * g


def _softplus(x):
    return jnp.maximum(x, 0.0) + jnp.log(1.0 + jnp.exp(-jnp.abs(x)))


def _params(semantics):
    return pltpu.CompilerParams(dimension_semantics=semantics, vmem_limit_bytes=VMEM_LIMIT_BYTES)


def _const_spec(shape):
    zeros = (0,) * len(shape)
    return pl.BlockSpec(shape, lambda *_: zeros)


def _norm_matmul_kernel(x_ref, g_ref, *refs):
    n_out = len(refs) // 2
    w_refs, o_refs = refs[:n_out], refs[n_out:]
    n = _rmsnorm(x_ref[...], g_ref[...]).astype(BF16)
    for w_ref, o_ref in zip(w_refs, o_refs):
        cols = w_ref.shape[1]
        for c0 in range(0, cols, COL_CHUNK):
            c1 = min(c0 + COL_CHUNK, cols)
            o_ref[:, c0:c1] = jnp.dot(n, w_ref[:, c0:c1], preferred_element_type=F32).astype(o_ref.dtype)


def _norm_matmul(x, g, ws, name):
    rows, k = x.shape
    tm = min(TOKEN_TILE, rows)
    return pl.pallas_call(
        _norm_matmul_kernel,
        grid=(rows // tm,),
        in_specs=[pl.BlockSpec((tm, k), lambda i: (i, 0)), _const_spec((1, k))]
        + [_const_spec(w.shape) for w in ws],
        out_specs=[pl.BlockSpec((tm, w.shape[1]), lambda i: (i, 0)) for w in ws],
        out_shape=[jax.ShapeDtypeStruct((rows, w.shape[1]), F32) for w in ws],
        compiler_params=_params(("parallel",)),
        name=name,
    )(x, g, *ws)


def _out_proj_kernel(x_ref, a_ref, b_ref, wa_ref, wb_ref, o_ref):
    acc = jnp.dot(a_ref[...], wa_ref[...], preferred_element_type=F32)
    acc = acc + jnp.dot(b_ref[...], wb_ref[...], preferred_element_type=F32)
    o_ref[...] = x_ref[...] + acc


def _out_proj(x, a, b, wa, wb):
    rows, d = x.shape
    tm = min(TOKEN_TILE, rows)
    row_spec = lambda width: pl.BlockSpec((tm, width), lambda i: (i, 0))
    return pl.pallas_call(
        _out_proj_kernel,
        grid=(rows // tm,),
        in_specs=[row_spec(d), row_spec(a.shape[1]), row_spec(b.shape[1]),
                  _const_spec(wa.shape), _const_spec(wb.shape)],
        out_specs=row_spec(d),
        out_shape=jax.ShapeDtypeStruct((rows, d), F32),
        compiler_params=_params(("parallel",)),
        name="out_proj",
    )(x, a, b, wa, wb)


def _hgrn_kernel(p_ref, s0_ref, lb_ref, gn_ref, o_ref, sn_ref, st_scr, *, chunk):
    n_blk = chunk // HG_BLOCK
    W = HG_WIDTH
    heads = [slice(h * HG_DIM, (h + 1) * HG_DIM) for h in range(HG_HEADS)]
    hr = range(HG_HEADS)

    @pl.when(pl.program_id(1) == 0)
    def _():
        for h in hr:
            st_scr[h] = s0_ref[0, h].T

    ti = lax.broadcasted_iota(jnp.int32, (chunk, chunk), 0)
    tj = lax.broadcasted_iota(jnp.int32, (chunk, chunk), 1)
    diag_mask = (ti >= tj) & (ti // HG_BLOCK == tj // HG_BLOCK)
    ri = lax.broadcasted_iota(jnp.int32, (3 * chunk, 3 * chunk), 0)
    ci = lax.broadcasted_iota(jnp.int32, (3 * chunk, 3 * chunk), 1) % chunk
    kind, rt = ri // chunk, ri % chunk
    same = rt // HG_BLOCK == ci // HG_BLOCK
    as_f32 = lambda m: m.astype(F32)
    cum_mask = jnp.where(kind == 0, as_f32(rt >= ci),
                         jnp.where(kind == 1, as_f32((rt >= ci) & same),
                                   as_f32(ci // HG_BLOCK <= rt // HG_BLOCK))).astype(BF16)
    row_blk = lax.broadcasted_iota(jnp.int32, (chunk, 1), 0) // HG_BLOCK
    lb = lb_ref[...]
    gn = gn_ref[...]

    def one_chunk(i, carry):
        off = pl.multiple_of(i * chunk, chunk)
        p = p_ref[0, pl.ds(off, chunk), :]
        q, fpre, v, gate = p[:, :W], p[:, W:2 * W], p[:, 2 * W:3 * W], p[:, 3 * W:]
        sig = jax.nn.sigmoid(fpre)
        logf = jnp.log(lb + (1.0 - lb) * sig)
        k = (1.0 - lb) * (1.0 - sig)
        cums = _cumdot(cum_mask, logf)
        lc, lblk, lend = cums[:chunk], cums[chunk:2 * chunk], cums[2 * chunk:]
        llast = lc[chunk - 1:chunk, :]

        q_in = q * jnp.exp(lc)
        qg = q * jnp.exp(lblk)
        kg = k * jnp.exp(-lblk)
        kd = k * jnp.exp(lend - lc)
        k_end = k * jnp.exp(llast - lc)
        g_chunk = jnp.exp(llast)
        q_cross, k_cross = [], []
        for j in range(n_blk - 1):
            end_j = lend[j * HG_BLOCK:j * HG_BLOCK + 1, :]
            q_cross.append(jnp.where(row_blk > j, q * jnp.exp(jnp.minimum(lc - end_j, 0.0)), 0.0).astype(BF16))
            k_cross.append(jnp.where(row_blk == j, kd, 0.0).astype(BF16))

        att = [jnp.where(diag_mask, _bdot(qg[:, sl], kg[:, sl], NT), 0.0) for sl in heads]
        if n_blk > 1:
            att = [att[h] + _bdot(jnp.concatenate([qc[:, heads[h]] for qc in q_cross], axis=1),
                                  jnp.concatenate([kc[:, heads[h]] for kc in k_cross], axis=1), NT) for h in hr]
        st = [st_scr[h] for h in hr]
        o = [_bdot(q_in[:, heads[h]], st[h], NT) + _bdot(att[h], v[:, heads[h]]) for h in hr]
        for h in hr:
            st_scr[h] = st[h] * g_chunk[:, heads[h]] + _bdot(v[:, heads[h]], k_end[:, heads[h]], TN)
        for h in hr:
            on = o[h] * lax.rsqrt(jnp.mean(o[h] * o[h], axis=-1, keepdims=True) + RMS_EPS)
            o_ref[0, pl.ds(off, chunk), heads[h]] = (
                on * gn[:, heads[h]] * jax.nn.silu(gate[:, heads[h]])).astype(o_ref.dtype)
        return carry

    lax.fori_loop(0, p_ref.shape[1] // chunk, one_chunk, 0)

    @pl.when(pl.program_id(1) == pl.num_programs(1) - 1)
    def _():
        for h in hr:
            sn_ref[0, h] = st_scr[h].T


def _hgrn(p, s0, lb, gn):
    bsz, t_len, _ = p.shape
    chunk = min(TIME_CHUNK, t_len)
    t_blk = min(TIME_BLOCK, t_len)
    return pl.pallas_call(
        functools.partial(_hgrn_kernel, chunk=chunk),
        grid=(bsz, t_len // t_blk),
        in_specs=[pl.BlockSpec((1, t_blk, HG_COLS), lambda b, c: (b, c, 0)),
                  pl.BlockSpec((1, HG_HEADS, HG_DIM, HG_DIM), lambda b, c: (b, 0, 0, 0)),
                  _const_spec((1, HG_WIDTH)), _const_spec((1, HG_WIDTH))],
        out_specs=[pl.BlockSpec((1, t_blk, HG_WIDTH), lambda b, c: (b, c, 0)),
                   pl.BlockSpec((1, HG_HEADS, HG_DIM, HG_DIM), lambda b, c: (b, 0, 0, 0))],
        out_shape=[jax.ShapeDtypeStruct((bsz, t_len, HG_WIDTH), BF16),
                   jax.ShapeDtypeStruct((bsz, HG_HEADS, HG_DIM, HG_DIM), F32)],
        scratch_shapes=[pltpu.VMEM((HG_HEADS, HG_DIM, HG_DIM), F32)],
        compiler_params=_params(("parallel", "arbitrary")),
        name="hgrn",
    )(p, s0, lb, gn)


def _rwkv_kernel(p_ref, s0_ref, sh0_ref, mu_ref, w0_ref, wb_ref, a0_ref, ab_ref, gb_ref, kk_ref, ka_ref,
                 rk_ref, gnw_ref, gnb_ref, j_ref, o_ref, sn_ref, shn_ref, s_scr, sh_scr, *, chunk):
    W, N, C = RW_WIDTH, RW_HEAD, chunk
    heads = [slice(h * N, (h + 1) * N) for h in range(RW_HEADS)]
    hr = range(RW_HEADS)

    @pl.when(pl.program_id(1) == 0)
    def _():
        s_scr[...] = s0_ref[0]
        sh_scr[...] = sh0_ref[0]

    row = lax.broadcasted_iota(jnp.int32, (C, 1), 0)
    ti = lax.broadcasted_iota(jnp.int32, (C, C), 0)
    tj = lax.broadcasted_iota(jnp.int32, (C, C), 1)
    incl = ti >= tj
    strict = ti > tj
    cum_mask = (lax.broadcasted_iota(jnp.int32, (C, 3 * C), 0)
                >= lax.broadcasted_iota(jnp.int32, (C, 3 * C), 1) % C).astype(BF16)
    jmat = j_ref[...]

    def one_chunk(i, carry):
        off = pl.multiple_of(i * C, C)
        p = p_ref[0, pl.ds(off, C), :]
        prev = jnp.where(row == 0, sh_scr[...], pltpu.roll(p, 1, 0))
        sh_scr[...] = p[C - 1:C, :]
        xs = p + (prev - p) * mu_ref[...]
        r, k, v = xs[:, :W], xs[:, W:2 * W], xs[:, 2 * W:3 * W]
        wd = xs[:, 3 * W:3 * W + RW_DECAY_LORA]
        ad = xs[:, 3 * W + RW_DECAY_LORA:3 * W + RW_DECAY_LORA + RW_A_LORA]
        gd = xs[:, 3 * W + RW_DECAY_LORA + RW_A_LORA:]

        w_log = -_softplus(-(w0_ref[...] + _bdot(jnp.tanh(wd), wb_ref[...]))) - 0.5
        ld = -jnp.exp(w_log)
        a_icl = jax.nn.sigmoid(a0_ref[...] + _bdot(ad, ab_ref[...]))
        gate = _bdot(jax.nn.sigmoid(gd), gb_ref[...])
        kk = k * kk_ref[...]
        kk = kk / jnp.maximum(jnp.sqrt(_segsum(kk * kk, jmat)), 1e-12)
        k2 = k * (1.0 + (a_icl - 1.0) * ka_ref[...])
        av = -kk
        bv = kk * a_icl

        lc = _cumdot(cum_mask, ld)
        llast = lc[C - 1:C, :]
        e_in = jnp.exp(lc)
        e_inv = jnp.exp(-lc)
        e_end = jnp.exp(llast - lc)
        a_t = av * jnp.exp(lc - ld)
        r_t = r * e_in
        b_t = bv * e_inv
        k_t = k2 * e_inv
        b_h = bv * e_end
        k_h = k2 * e_end
        g_chunk = jnp.exp(llast)

        g = [_bdot(jnp.concatenate([a_t[:, sl], r_t[:, sl]], axis=0),
                   jnp.concatenate([b_t[:, sl], k_t[:, sl]], axis=0), NT) for sl in heads]
        l_pow = [jnp.where(strict, g[h][:C, :C], 0.0) for h in hr]
        l_ak = [jnp.where(strict, g[h][:C, C:], 0.0) for h in hr]
        a_rb = [jnp.where(incl, g[h][C:, :C], 0.0) for h in hr]
        a_rk = [jnp.where(incl, g[h][C:, C:], 0.0) for h in hr]
        vh = [v[:, sl] for sl in heads]
        au = [jnp.concatenate([a_t[:, heads[h]], _bdot(l_ak[h], vh[h])], axis=1) for h in hr]
        span = 1
        while True:
            au = [au[h] + _bdot(l_pow[h], au[h]) for h in hr]
            span *= 2
            if span >= C:
                break
            l_pow = [_bdot(l, l) for l in l_pow]
        mn = [_bdot(au[h], b_h[:, heads[h]], TN) for h in hr]
        nt = [mn[h][N:] + _bdot(vh[h], k_h[:, heads[h]], TN) for h in hr]
        ry = [_bdot(a_rb[h], au[h]) for h in hr]
        y0 = [ry[h][:, N:] + _bdot(a_rk[h], vh[h]) for h in hr]
        s_old = [s_scr[h] for h in hr]
        ys = [_bdot(r_t[:, heads[h]] + ry[h][:, :N], s_old[h], NT) + y0[h] for h in hr]
        for h in hr:
            s_scr[h] = s_old[h] * g_chunk[:, heads[h]] + _bdot(s_old[h], mn[h][:N]) + nt[h]

        y = jnp.concatenate(ys, axis=1)
        mean = _segsum(y, jmat) * (1.0 / N)
        yc = y - mean
        var = _segsum(yc * yc, jmat) * (1.0 / N)
        yn = yc * lax.rsqrt(var + GN_EPS) * gnw_ref[...] + gnb_ref[...]
        bonus = _segsum(r * k2 * rk_ref[...], jmat) * v
        o_ref[0, pl.ds(off, C), :] = ((yn + bonus) * gate).astype(o_ref.dtype)
        return carry

    lax.fori_loop(0, p_ref.shape[1] // C, one_chunk, 0)

    @pl.when(pl.program_id(1) == pl.num_programs(1) - 1)
    def _():
        sn_ref[0] = s_scr[...]
        shn_ref[0] = sh_scr[...]


def _rwkv(p, s0, sh0, vecs, wb, ab, gb, jmat):
    bsz, t_len, _ = p.shape
    chunk = min(TIME_CHUNK, t_len)
    t_blk = min(TIME_BLOCK, t_len)
    mu, w0, a0, kk, ka, rk, gnw, gnb = vecs
    state_spec = pl.BlockSpec((1, RW_HEADS, RW_HEAD, RW_HEAD), lambda b, c: (b, 0, 0, 0))
    shift_spec = pl.BlockSpec((1, 1, RW_COLS), lambda b, c: (b, 0, 0))
    return pl.pallas_call(
        functools.partial(_rwkv_kernel, chunk=chunk),
        grid=(bsz, t_len // t_blk),
        in_specs=[pl.BlockSpec((1, t_blk, RW_COLS), lambda b, c: (b, c, 0)), state_spec, shift_spec,
                  _const_spec(mu.shape), _const_spec(w0.shape), _const_spec(wb.shape), _const_spec(a0.shape),
                  _const_spec(ab.shape), _const_spec(gb.shape), _const_spec(kk.shape), _const_spec(ka.shape),
                  _const_spec(rk.shape), _const_spec(gnw.shape), _const_spec(gnb.shape), _const_spec(jmat.shape)],
        out_specs=[pl.BlockSpec((1, t_blk, RW_WIDTH), lambda b, c: (b, c, 0)), state_spec, shift_spec],
        out_shape=[jax.ShapeDtypeStruct((bsz, t_len, RW_WIDTH), BF16),
                   jax.ShapeDtypeStruct((bsz, RW_HEADS, RW_HEAD, RW_HEAD), F32),
                   jax.ShapeDtypeStruct((bsz, 1, RW_COLS), F32)],
        scratch_shapes=[pltpu.VMEM((RW_HEADS, RW_HEAD, RW_HEAD), F32), pltpu.VMEM((1, RW_COLS), F32)],
        compiler_params=_params(("parallel", "arbitrary")),
        name="rwkv",
    )(p, s0, sh0, mu, w0, wb, a0, ab, gb, kk, ka, rk, gnw, gnb, jmat)


def _cross_kernel(x_ref, k_ref, v_ref, g_ref, wq_ref, wo_ref, o_ref):
    x = x_ref[0]
    q = jnp.dot(_rmsnorm(x, g_ref[...]).astype(BF16), wq_ref[...], preferred_element_type=F32)
    heads = []
    for h in range(X_HEADS):
        sl = slice(h * X_DIM, (h + 1) * X_DIM)
        s = _bdot(q[:, sl], k_ref[0, :, sl], NT) * (X_DIM ** -0.5)
        e = jnp.exp(s - jnp.max(s, axis=-1, keepdims=True))
        pr = e / jnp.sum(e, axis=-1, keepdims=True)
        heads.append(_bdot(pr, v_ref[0, :, sl]).astype(BF16))
    o_ref[0] = x + jnp.dot(jnp.concatenate(heads, axis=1), wo_ref[...], preferred_element_type=F32)


def _cross(x, mk, mv, g, wq, wo):
    bsz, t_len, d = x.shape
    tq = min(TOKEN_TILE, t_len)
    x_spec = pl.BlockSpec((1, tq, d), lambda b, t: (b, t, 0))
    mem_spec = pl.BlockSpec((1, N_MEM, d), lambda b, t: (b, 0, 0))
    return pl.pallas_call(
        _cross_kernel,
        grid=(bsz, t_len // tq),
        in_specs=[x_spec, mem_spec, mem_spec, _const_spec((1, d)), _const_spec(wq.shape), _const_spec(wo.shape)],
        out_specs=x_spec,
        out_shape=jax.ShapeDtypeStruct(x.shape, F32),
        compiler_params=_params(("parallel", "parallel")),
        name="cross",
    )(x, mk, mv, g, wq, wo)


def _ffn_kernel(x_ref, g_ref, w1_ref, w3_ref, w2_ref, gf_ref, o_ref):
    x = x_ref[...]
    n = _rmsnorm(x, g_ref[...]).astype(BF16)
    acc = x
    for c0 in range(0, D_FF, FF_CHUNK):
        c1 = c0 + FF_CHUNK
        up = jnp.dot(n, w1_ref[:, c0:c1], preferred_element_type=F32)
        lin = jnp.dot(n, w3_ref[:, c0:c1], preferred_element_type=F32)
        hid = (jax.nn.silu(up) * lin).astype(BF16)
        acc = acc + jnp.dot(hid, w2_ref[c0:c1, :], preferred_element_type=F32)
    o_ref[...] = _rmsnorm(acc, gf_ref[...])


def _ffn(x, g, w1, w3, w2, gf):
    rows, d = x.shape
    tm = min(TOKEN_TILE, rows)
    row_spec = pl.BlockSpec((tm, d), lambda i: (i, 0))
    return pl.pallas_call(
        _ffn_kernel,
        grid=(rows // tm,),
        in_specs=[row_spec, _const_spec((1, d)), _const_spec(w1.shape), _const_spec(w3.shape),
                  _const_spec(w2.shape), _const_spec((1, d))],
        out_specs=row_spec,
        out_shape=jax.ShapeDtypeStruct((rows, d), F32),
        compiler_params=_params(("parallel",)),
        name="ffn",
    )(x, g, w1, w3, w2, gf)


def _trunk(x, mk, mv, hg0, rw0, sh0, lb, w):
    bsz, t_len, d = x.shape
    x2d = x.reshape(bsz * t_len, d)
    p_hg, p_rw = _norm_matmul(x2d, w['norm_mix'], [w['w_in_hg'], w['w_in_rw']], "in_proj")
    hg_o, hg_s = _hgrn(p_hg.reshape(bsz, t_len, HG_COLS), hg0, lb, w['hgrn_norm'])
    rw_o, rw_s, rw_sh = _rwkv(p_rw.reshape(bsz, t_len, RW_COLS), rw0, sh0, w['rw_vecs'],
                              w['rw_w_b'], w['rw_a_b'], w['rw_g_b'], w['jmat'])
    x2d = _out_proj(x2d, hg_o.reshape(bsz * t_len, HG_WIDTH), rw_o.reshape(bsz * t_len, RW_WIDTH),
                    w['w_out_hg'], w['w_out_rw'])
    x3d = _cross(x2d.reshape(bsz, t_len, d), mk, mv, w['norm_cross'], w['w_cq'], w['w_co'])
    y = _ffn(x3d.reshape(bsz * t_len, d), w['norm_ffn'], w['w_ff1'], w['w_ff3'], w['w_ff2'], w['norm_final'])
    return y.reshape(bsz, t_len, d), hg_s, rw_s, rw_sh


def kernel(x_prompt, mem_prompt, x_sample, cache_mem_k, cache_mem_v, state_hgrn, state_rwkv, state_rwkv_shift, hgrn_lb_logits, norm_mix, w_in, hgrn_norm, rw_mu, rw_w0, rw_w_b, rw_a0, rw_a_b, rw_g_b, rw_k_k, rw_k_a, rw_r_k, rw_gn_w, rw_gn_b, w_out, norm_cross, norm_mem, w_cq, w_ck, w_cv, w_co, norm_ffn, w_ff1, w_ff3, w_ff2, norm_final):
    assert w_in.shape[0] == 1, "single-layer trunk"
    bp = x_prompt.shape[0]
    row = lambda a: a.reshape(1, -1).astype(F32)
    bf = lambda a: a.astype(BF16)
    lb = jnp.cumsum(jax.nn.softmax(hgrn_lb_logits.astype(F32), axis=0), axis=0)[0:1]
    head_id = jnp.arange(RW_WIDTH, dtype=jnp.int32) // RW_HEAD
    w = {
        'norm_mix': row(norm_mix[0]), 'w_in_hg': bf(w_in[0, :, :HG_COLS]), 'w_in_rw': bf(w_in[0, :, HG_COLS:]),
        'hgrn_norm': row(hgrn_norm[0]),
        'rw_vecs': tuple(row(a[0]) for a in (rw_mu, rw_w0, rw_a0, rw_k_k, rw_k_a, rw_r_k, rw_gn_w, rw_gn_b)),
        'rw_w_b': bf(rw_w_b[0]), 'rw_a_b': bf(rw_a_b[0]), 'rw_g_b': bf(rw_g_b[0]),
        'jmat': (head_id[:, None] == head_id[None, :]).astype(BF16),
        'w_out_hg': bf(w_out[0, :HG_WIDTH]), 'w_out_rw': bf(w_out[0, HG_WIDTH:]),
        'norm_cross': row(norm_cross[0]), 'w_cq': bf(w_cq[0]), 'w_co': bf(w_co[0]),
        'norm_ffn': row(norm_ffn[0]), 'w_ff1': bf(w_ff1[0]), 'w_ff3': bf(w_ff3[0]), 'w_ff2': bf(w_ff2[0]),
        'norm_final': row(norm_final),
    }
    mk, mv = _norm_matmul(mem_prompt.reshape(bp * N_MEM, D_MODEL), row(norm_mem[0]),
                          [bf(w_ck[0]), bf(w_cv[0])], "mem_kv")
    mk = mk.reshape(bp, N_MEM, D_MODEL)
    mv = mv.reshape(bp, N_MEM, D_MODEL)
    y_p, p_hg, p_rw, p_sh = _trunk(
        x_prompt, mk, mv, jnp.zeros((bp, HG_HEADS, HG_DIM, HG_DIM), F32),
        jnp.zeros((bp, RW_HEADS, RW_HEAD, RW_HEAD), F32), jnp.zeros((bp, 1, RW_COLS), F32), lb, w)
    bs = x_sample.shape[0]
    y_s, s_hg, s_rw, s_sh = _trunk(
        x_sample, cache_mem_k[0].reshape(bs, N_MEM, D_MODEL), cache_mem_v[0].reshape(bs, N_MEM, D_MODEL),
        state_hgrn[0], state_rwkv[0], state_rwkv_shift[0], lb, w)
    return (y_p, y_s, p_hg[None], p_rw[None], p_sh[None],
            mk.reshape(1, bp, N_MEM, X_HEADS, X_DIM), mv.reshape(1, bp, N_MEM, X_HEADS, X_DIM),
            s_hg[None], s_rw[None], s_sh[None])
```

```python
import functools

import jax
import jax.numpy as jnp
from jax import lax
from jax.experimental import pallas as pl
from jax.experimental.pallas import tpu as pltpu

F32 = jnp.float32
BF16 = jnp.bfloat16

D_MODEL = 1024
HG_WIDTH = 512
HG_HEADS = 4
HG_DIM = 128
HG_BLOCK = 16
HG_COLS = 4 * HG_WIDTH
RW_WIDTH = 512
RW_HEAD = 64
RW_HEADS = 8
RW_DECAY_LORA = 64
RW_A_LORA = 64
RW_GATE_LORA = 128
RW_COLS = 3 * RW_WIDTH + RW_DECAY_LORA + RW_A_LORA + RW_GATE_LORA
N_MEM = 256
X_HEADS = 4
X_DIM = D_MODEL // X_HEADS
D_FF = 2816
RMS_EPS = 1e-6
GN_EPS = 64e-5

TIME_CHUNK = 64
TIME_BLOCK = 256
RW_CHUNK_GROUP = 2
TOKEN_TILE = 512
FF_CHUNK = 1408
COL_CHUNK = 1024
VMEM_LIMIT_BYTES = 56 * 1024 * 1024

NN = (((1,), (0,)), ((), ()))
NT = (((1,), (1,)), ((), ()))
TN = (((0,), (0,)), ((), ()))


def _bdot(a, b, dims=NN):
    return lax.dot_general(a.astype(BF16), b.astype(BF16), dims, preferred_element_type=F32)


def _split3(x):
    hi = x.astype(BF16)
    r1 = x - hi.astype(F32)
    mid = r1.astype(BF16)
    lo = (r1 - mid.astype(F32)).astype(BF16)
    return hi, mid, lo


def _cumdot(mask3, x):
    return jnp.dot(mask3, jnp.concatenate(_split3(x), axis=0), preferred_element_type=F32)


def _segsum(x, j_bf16):
    rows = x.shape[0]
    hi = x.astype(BF16)
    lo = (x - hi.astype(F32)).astype(BF16)
    both = jnp.dot(jnp.concatenate([hi, lo], axis=0), j_bf16, preferred_element_type=F32)
    return both[:rows] + both[rows:]


def _rmsnorm(x, g):
    return x * lax.rsqrt(jnp.mean(x * x, axis=-1, keepdims=True) + RMS_EPS) * g


def _softplus(x):
    return jnp.maximum(x, 0.0) + jnp.log(1.0 + jnp.exp(-jnp.abs(x)))


def _params(semantics):
    return pltpu.CompilerParams(dimension_semantics=semantics, vmem_limit_bytes=VMEM_LIMIT_BYTES)


def _const_spec(shape):
    zeros = (0,) * len(shape)
    return pl.BlockSpec(shape, lambda *_: zeros)


def _norm_matmul_kernel(x_ref, g_ref, *refs):
    n_out = len(refs) // 2
    w_refs, o_refs = refs[:n_out], refs[n_out:]
    n = _rmsnorm(x_ref[...], g_ref[...]).astype(BF16)
    for w_ref, o_ref in zip(w_refs, o_refs):
        cols = w_ref.shape[1]
        for c0 in range(0, cols, COL_CHUNK):
            c1 = min(c0 + COL_CHUNK, cols)
            o_ref[:, c0:c1] = jnp.dot(n, w_ref[:, c0:c1], preferred_element_type=F32).astype(o_ref.dtype)


def _norm_matmul(x, g, ws, name):
    rows, k = x.shape
    tm = min(TOKEN_TILE, rows)
    return pl.pallas_call(
        _norm_matmul_kernel,
        grid=(rows // tm,),
        in_specs=[pl.BlockSpec((tm, k), lambda i: (i, 0)), _const_spec((1, k))]
        + [_const_spec(w.shape) for w in ws],
        out_specs=[pl.BlockSpec((tm, w.shape[1]), lambda i: (i, 0)) for w in ws],
        out_shape=[jax.ShapeDtypeStruct((rows, w.shape[1]), F32) for w in ws],
        compiler_params=_params(("parallel",)),
        name=name,
    )(x, g, *ws)


def _out_proj_kernel(x_ref, a_ref, b_ref, wa_ref, wb_ref, o_ref):
    acc = jnp.dot(a_ref[...], wa_ref[...], preferred_element_type=F32)
    acc = acc + jnp.dot(b_ref[...], wb_ref[...], preferred_element_type=F32)
    o_ref[...] = x_ref[...] + acc


def _out_proj(x, a, b, wa, wb):
    rows, d = x.shape
    tm = min(TOKEN_TILE, rows)
    row_spec = lambda width: pl.BlockSpec((tm, width), lambda i: (i, 0))
    return pl.pallas_call(
        _out_proj_kernel,
        grid=(rows // tm,),
        in_specs=[row_spec(d), row_spec(a.shape[1]), row_spec(b.shape[1]),
                  _const_spec(wa.shape), _const_spec(wb.shape)],
        out_specs=row_spec(d),
        out_shape=jax.ShapeDtypeStruct((rows, d), F32),
        compiler_params=_params(("parallel",)),
        name="out_proj",
    )(x, a, b, wa, wb)


def _hgrn_kernel(p_ref, s0_ref, lb_ref, gn_ref, o_ref, sn_ref, st_scr, *, chunk):
    n_blk = chunk // HG_BLOCK
    W = HG_WIDTH
    heads = [slice(h * HG_DIM, (h + 1) * HG_DIM) for h in range(HG_HEADS)]
    hr = range(HG_HEADS)

    @pl.when(pl.program_id(1) == 0)
    def _():
        for h in hr:
            st_scr[h] = s0_ref[0, h].T

    ti = lax.broadcasted_iota(jnp.int32, (chunk, chunk), 0)
    tj = lax.broadcasted_iota(jnp.int32, (chunk, chunk), 1)
    diag_mask = (ti >= tj) & (ti // HG_BLOCK == tj // HG_BLOCK)
    ri = lax.broadcasted_iota(jnp.int32, (3 * chunk, 3 * chunk), 0)
    ci = lax.broadcasted_iota(jnp.int32, (3 * chunk, 3 * chunk), 1) % chunk
    kind, rt = ri // chunk, ri % chunk
    same = rt // HG_BLOCK == ci // HG_BLOCK
    as_f32 = lambda m: m.astype(F32)
    cum_mask = jnp.where(kind == 0, as_f32(rt >= ci),
                         jnp.where(kind == 1, as_f32((rt >= ci) & same),
                                   as_f32(ci // HG_BLOCK <= rt // HG_BLOCK))).astype(BF16)
    row_blk = lax.broadcasted_iota(jnp.int32, (chunk, 1), 0) // HG_BLOCK
    lb = lb_ref[...]
    gn = gn_ref[...]

    def one_chunk(i, carry):
        off = pl.multiple_of(i * chunk, chunk)
        p = p_ref[0, pl.ds(off, chunk), :]
        q, fpre, v, gate = p[:, :W], p[:, W:2 * W], p[:, 2 * W:3 * W], p[:, 3 * W:]
        sig = jax.nn.sigmoid(fpre)
        logf = jnp.log(lb + (1.0 - lb) * sig)
        k = (1.0 - lb) * (1.0 - sig)
        cums = _cumdot(cum_mask, logf)
        lc, lblk, lend = cums[:chunk], cums[chunk:2 * chunk], cums[2 * chunk:]
        llast = lc[chunk - 1:chunk, :]

        q_in = q * jnp.exp(lc)
        qg = q * jnp.exp(lblk)
        kg = k * jnp.exp(-lblk)
        kd = k * jnp.exp(lend - lc)
        k_end = k * jnp.exp(llast - lc)
        g_chunk = jnp.exp(llast)
        q_cross, k_cross = [], []
        for j in range(n_blk - 1):
            end_j = lend[j * HG_BLOCK:j * HG_BLOCK + 1, :]
            q_cross.append(jnp.where(row_blk > j, q * jnp.exp(jnp.minimum(lc - end_j, 0.0)), 0.0).astype(BF16))
            k_cross.append(jnp.where(row_blk == j, kd, 0.0).astype(BF16))

        att = [jnp.where(diag_mask, _bdot(qg[:, sl], kg[:, sl], NT), 0.0) for sl in heads]
        if n_blk > 1:
            att = [att[h] + _bdot(jnp.concatenate([qc[:, heads[h]] for qc in q_cross], axis=1),
                                  jnp.concatenate([kc[:, heads[h]] for kc in k_cross], axis=1), NT) for h in hr]
        st = [st_scr[h] for h in hr]
        o = [_bdot(q_in[:, heads[h]], st[h], NT) + _bdot(att[h], v[:, heads[h]]) for h in hr]
        for h in hr:
            st_scr[h] = st[h] * g_chunk[:, heads[h]] + _bdot(v[:, heads[h]], k_end[:, heads[h]], TN)
        for h in hr:
            on = o[h] * lax.rsqrt(jnp.mean(o[h] * o[h], axis=-1, keepdims=True) + RMS_EPS)
            o_ref[0, pl.ds(off, chunk), heads[h]] = (
                on * gn[:, heads[h]] * jax.nn.silu(gate[:, heads[h]])).astype(o_ref.dtype)
        return carry

    lax.fori_loop(0, p_ref.shape[1] // chunk, one_chunk, 0)

    @pl.when(pl.program_id(1) == pl.num_programs(1) - 1)
    def _():
        for h in hr:
            sn_ref[0, h] = st_scr[h].T


def _hgrn(p, s0, lb, gn):
    bsz, t_len, _ = p.shape
    chunk = min(TIME_CHUNK, t_len)
    t_blk = min(TIME_BLOCK, t_len)
    return pl.pallas_call(
        functools.partial(_hgrn_kernel, chunk=chunk),
        grid=(bsz, t_len // t_blk),
        in_specs=[pl.BlockSpec((1, t_blk, HG_COLS), lambda b, c: (b, c, 0)),
                  pl.BlockSpec((1, HG_HEADS, HG_DIM, HG_DIM), lambda b, c: (b, 0, 0, 0)),
                  _const_spec((1, HG_WIDTH)), _const_spec((1, HG_WIDTH))],
        out_specs=[pl.BlockSpec((1, t_blk, HG_WIDTH), lambda b, c: (b, c, 0)),
                   pl.BlockSpec((1, HG_HEADS, HG_DIM, HG_DIM), lambda b, c: (b, 0, 0, 0))],
        out_shape=[jax.ShapeDtypeStruct((bsz, t_len, HG_WIDTH), BF16),
                   jax.ShapeDtypeStruct((bsz, HG_HEADS, HG_DIM, HG_DIM), F32)],
        scratch_shapes=[pltpu.VMEM((HG_HEADS, HG_DIM, HG_DIM), F32)],
        compiler_params=_params(("parallel", "arbitrary")),
        name="hgrn",
    )(p, s0, lb, gn)


def _rwkv_kernel(p_ref, s0_ref, sh0_ref, mu_ref, w0_ref, wb_ref, a0_ref, ab_ref, gb_ref, kk_ref, ka_ref,
                 rk_ref, gnw_ref, gnb_ref, j_ref, o_ref, sn_ref, shn_ref, s_scr, sh_scr, *, chunk):
    W, N, C = RW_WIDTH, RW_HEAD, chunk
    heads = [slice(h * N, (h + 1) * N) for h in range(RW_HEADS)]
    hr = range(RW_HEADS)

    @pl.when(pl.program_id(1) == 0)
    def _():
        s_scr[...] = s0_ref[0]
        sh_scr[...] = sh0_ref[0]

    n_chunks = p_ref.shape[1] // C
    group = RW_CHUNK_GROUP if n_chunks % RW_CHUNK_GROUP == 0 else 1
    R = group * C
    cr = range(group)
    units = [(ci, h) for ci in cr for h in hr]
    ur = range(len(units))
    row = lax.broadcasted_iota(jnp.int32, (R, 1), 0)
    ti = lax.broadcasted_iota(jnp.int32, (C, C), 0)
    tj = lax.broadcasted_iota(jnp.int32, (C, C), 1)
    incl = ti >= tj
    strict = ti > tj
    mi = lax.broadcasted_iota(jnp.int32, (R, 3 * R), 0)
    mj = lax.broadcasted_iota(jnp.int32, (R, 3 * R), 1) % R
    cum_mask = ((mi >= mj) & (mi // C == mj // C)).astype(BF16)
    jmat = j_ref[...]

    def part(x, unit):
        ci, h = unit
        return x[ci * C:(ci + 1) * C, heads[h]]

    def one_group(i, carry):
        off = pl.multiple_of(i * R, R)
        p = p_ref[0, pl.ds(off, R), :]
        prev = jnp.where(row == 0, sh_scr[...], pltpu.roll(p, 1, 0))
        sh_scr[...] = p[R - 1:R, :]
        xs = p + (prev - p) * mu_ref[...]
        r, k, v = xs[:, :W], xs[:, W:2 * W], xs[:, 2 * W:3 * W]
        wd = xs[:, 3 * W:3 * W + RW_DECAY_LORA]
        ad = xs[:, 3 * W + RW_DECAY_LORA:3 * W + RW_DECAY_LORA + RW_A_LORA]
        gd = xs[:, 3 * W + RW_DECAY_LORA + RW_A_LORA:]

        w_log = -_softplus(-(w0_ref[...] + _bdot(jnp.tanh(wd), wb_ref[...]))) - 0.5
        ld = -jnp.exp(w_log)
        a_icl = jax.nn.sigmoid(a0_ref[...] + _bdot(ad, ab_ref[...]))
        gate = _bdot(jax.nn.sigmoid(gd), gb_ref[...])
        kk = k * kk_ref[...]
        kk = kk / jnp.maximum(jnp.sqrt(_segsum(kk * kk, jmat)), 1e-12)
        k2 = k * (1.0 + (a_icl - 1.0) * ka_ref[...])
        av = -kk
        bv = kk * a_icl

        lc = _cumdot(cum_mask, ld)
        llast = [lc[ci * C + C - 1:ci * C + C, :] for ci in cr]
        to_end = jnp.concatenate([llast[ci] - lc[ci * C:(ci + 1) * C] for ci in cr], axis=0)
        e_in = jnp.exp(lc)
        e_inv = jnp.exp(-lc)
        e_end = jnp.exp(to_end)
        a_t = av * jnp.exp(lc - ld)
        r_t = r * e_in
        b_t = bv * e_inv
        k_t = k2 * e_inv
        b_h = bv * e_end
        k_h = k2 * e_end
        g_chunk = [jnp.exp(llast[ci]) for ci in cr]

        g = [_bdot(jnp.concatenate([part(a_t, u), part(r_t, u)], axis=0),
                   jnp.concatenate([part(b_t, u), part(k_t, u)], axis=0), NT) for u in units]
        l_pow = [jnp.where(strict, g[n][:C, :C], 0.0) for n in ur]
        l_ak = [jnp.where(strict, g[n][:C, C:], 0.0) for n in ur]
        a_rb = [jnp.where(incl, g[n][C:, :C], 0.0) for n in ur]
        a_rk = [jnp.where(incl, g[n][C:, C:], 0.0) for n in ur]
        vh = [part(v, u) for u in units]
        au = [jnp.concatenate([part(a_t, units[n]), _bdot(l_ak[n], vh[n])], axis=1) for n in ur]
        span = 1
        while True:
            au = [au[n] + _bdot(l_pow[n], au[n]) for n in ur]
            span *= 2
            if span >= C:
                break
            l_pow = [_bdot(l, l) for l in l_pow]
        mn = [_bdot(au[n], part(b_h, units[n]), TN) for n in ur]
        nt = [mn[n][N:] + _bdot(vh[n], part(k_h, units[n]), TN) for n in ur]
        ry = [_bdot(a_rb[n], au[n]) for n in ur]
        y0 = [ry[n][:, N:] + _bdot(a_rk[n], vh[n]) for n in ur]
        r_hat = [part(r_t, units[n]) + ry[n][:, :N] for n in ur]
        s_cur = [s_scr[h] for h in hr]
        ys = []
        for ci in cr:
            base = ci * RW_HEADS
            ys.append([_bdot(r_hat[base + h], s_cur[h], NT) + y0[base + h] for h in hr])
            s_cur = [s_cur[h] * g_chunk[ci][:, heads[h]] + _bdot(s_cur[h], mn[base + h][:N]) + nt[base + h]
                     for h in hr]
        for h in hr:
            s_scr[h] = s_cur[h]

        y = jnp.concatenate([jnp.concatenate(ys[ci], axis=1) for ci in cr], axis=0)
        mean = _segsum(y, jmat) * (1.0 / N)
        yc = y - mean
        var = _segsum(yc * yc, jmat) * (1.0 / N)
        yn = yc * lax.rsqrt(var + GN_EPS) * gnw_ref[...] + gnb_ref[...]
        bonus = _segsum(r * k2 * rk_ref[...], jmat) * v
        o_ref[0, pl.ds(off, R), :] = ((yn + bonus) * gate).astype(o_ref.dtype)
        return carry

    lax.fori_loop(0, n_chunks // group, one_group, 0)

    @pl.when(pl.program_id(1) == pl.num_programs(1) - 1)
    def _():
        sn_ref[0] = s_scr[...]
        shn_ref[0] = sh_scr[...]


def _rwkv(p, s0, sh0, vecs, wb, ab, gb, jmat):
    bsz, t_len, _ = p.shape
    chunk = min(TIME_CHUNK, t_len)
    t_blk = min(TIME_BLOCK, t_len)
    mu, w0, a0, kk, ka, rk, gnw, gnb = vecs
    state_spec = pl.BlockSpec((1, RW_HEADS, RW_HEAD, RW_HEAD), lambda b, c: (b, 0, 0, 0))
    shift_spec = pl.BlockSpec((1, 1, RW_COLS), lambda b, c: (b, 0, 0))
    return pl.pallas_call(
        functools.partial(_rwkv_kernel, chunk=chunk),
        grid=(bsz, t_len // t_blk),
        in_specs=[pl.BlockSpec((1, t_blk, RW_COLS), lambda b, c: (b, c, 0)), state_spec, shift_spec,
                  _const_spec(mu.shape), _const_spec(w0.shape), _const_spec(wb.shape), _const_spec(a0.shape),
                  _const_spec(ab.shape), _const_spec(gb.shape), _const_spec(kk.shape), _const_spec(ka.shape),
                  _const_spec(rk.shape), _const_spec(gnw.shape), _const_spec(gnb.shape), _const_spec(jmat.shape)],
        out_specs=[pl.BlockSpec((1, t_blk, RW_WIDTH), lambda b, c: (b, c, 0)), state_spec, shift_spec],
        out_shape=[jax.ShapeDtypeStruct((bsz, t_len, RW_WIDTH), BF16),
                   jax.ShapeDtypeStruct((bsz, RW_HEADS, RW_HEAD, RW_HEAD), F32),
                   jax.ShapeDtypeStruct((bsz, 1, RW_COLS), F32)],
        scratch_shapes=[pltpu.VMEM((RW_HEADS, RW_HEAD, RW_HEAD), F32), pltpu.VMEM((1, RW_COLS), F32)],
        compiler_params=_params(("parallel", "arbitrary")),
        name="rwkv",
    )(p, s0, sh0, mu, w0, wb, a0, ab, gb, kk, ka, rk, gnw, gnb, jmat)


def _cross_kernel(x_ref, k_ref, v_ref, g_ref, wq_ref, wo_ref, o_ref):
    x = x_ref[0]
    q = jnp.dot(_rmsnorm(x, g_ref[...]).astype(BF16), wq_ref[...], preferred_element_type=F32)
    heads = []
    for h in range(X_HEADS):
        sl = slice(h * X_DIM, (h + 1) * X_DIM)
        s = _bdot(q[:, sl], k_ref[0, :, sl], NT) * (X_DIM ** -0.5)
        e = jnp.exp(s - jnp.max(s, axis=-1, keepdims=True))
        pr = e / jnp.sum(e, axis=-1, keepdims=True)
        heads.append(_bdot(pr, v_ref[0, :, sl]).astype(BF16))
    o_ref[0] = x + jnp.dot(jnp.concatenate(heads, axis=1), wo_ref[...], preferred_element_type=F32)


def _cross(x, mk, mv, g, wq, wo):
    bsz, t_len, d = x.shape
    tq = min(TOKEN_TILE, t_len)
    x_spec = pl.BlockSpec((1, tq, d), lambda b, t: (b, t, 0))
    mem_spec = pl.BlockSpec((1, N_MEM, d), lambda b, t: (b, 0, 0))
    return pl.pallas_call(
        _cross_kernel,
        grid=(bsz, t_len // tq),
        in_specs=[x_spec, mem_spec, mem_spec, _const_spec((1, d)), _const_spec(wq.shape), _const_spec(wo.shape)],
        out_specs=x_spec,
        out_shape=jax.ShapeDtypeStruct(x.shape, F32),
        compiler_params=_params(("parallel", "parallel")),
        name="cross",
    )(x, mk, mv, g, wq, wo)


def _ffn_kernel(x_ref, g_ref, w1_ref, w3_ref, w2_ref, gf_ref, o_ref):
    x = x_ref[...]
    n = _rmsnorm(x, g_ref[...]).astype(BF16)
    acc = x
    for c0 in range(0, D_FF, FF_CHUNK):
        c1 = c0 + FF_CHUNK
        up = jnp.dot(n, w1_ref[:, c0:c1], preferred_element_type=F32)
        lin = jnp.dot(n, w3_ref[:, c0:c1], preferred_element_type=F32)
        hid = (jax.nn.silu(up) * lin).astype(BF16)
        acc = acc + jnp.dot(hid, w2_ref[c0:c1, :], preferred_element_type=F32)
    o_ref[...] = _rmsnorm(acc, gf_ref[...])


def _ffn(x, g, w1, w3, w2, gf):
    rows, d = x.shape
    tm = min(TOKEN_TILE, rows)
    row_spec = pl.BlockSpec((tm, d), lambda i: (i, 0))
    return pl.pallas_call(
        _ffn_kernel,
        grid=(rows // tm,),
        in_specs=[row_spec, _const_spec((1, d)), _const_spec(w1.shape), _const_spec(w3.shape),
                  _const_spec(w2.shape), _const_spec((1, d))],
        out_specs=row_spec,
        out_shape=jax.ShapeDtypeStruct((rows, d), F32),
        compiler_params=_params(("parallel",)),
        name="ffn",
    )(x, g, w1, w3, w2, gf)


def _trunk(x, mk, mv, hg0, rw0, sh0, lb, w):
    bsz, t_len, d = x.shape
    x2d = x.reshape(bsz * t_len, d)
    p_hg, p_rw = _norm_matmul(x2d, w['norm_mix'], [w['w_in_hg'], w['w_in_rw']], "in_proj")
    hg_o, hg_s = _hgrn(p_hg.reshape(bsz, t_len, HG_COLS), hg0, lb, w['hgrn_norm'])
    rw_o, rw_s, rw_sh = _rwkv(p_rw.reshape(bsz, t_len, RW_COLS), rw0, sh0, w['rw_vecs'],
                              w['rw_w_b'], w['rw_a_b'], w['rw_g_b'], w['jmat'])
    x2d = _out_proj(x2d, hg_o.reshape(bsz * t_len, HG_WIDTH), rw_o.reshape(bsz * t_len, RW_WIDTH),
                    w['w_out_hg'], w['w_out_rw'])
    x3d = _cross(x2d.reshape(bsz, t_len, d), mk, mv, w['norm_cross'], w['w_cq'], w['w_co'])
    y = _ffn(x3d.reshape(bsz * t_len, d), w['norm_ffn'], w['w_ff1'], w['w_ff3'], w['w_ff2'], w['norm_final'])
    return y.reshape(bsz, t_len, d), hg_s, rw_s, rw_sh


def kernel(x_prompt, mem_prompt, x_sample, cache_mem_k, cache_mem_v, state_hgrn, state_rwkv, state_rwkv_shift, hgrn_lb_logits, norm_mix, w_in, hgrn_norm, rw_mu, rw_w0, rw_w_b, rw_a0, rw_a_b, rw_g_b, rw_k_k, rw_k_a, rw_r_k, rw_gn_w, rw_gn_b, w_out, norm_cross, norm_mem, w_cq, w_ck, w_cv, w_co, norm_ffn, w_ff1, w_ff3, w_ff2, norm_final):
    assert w_in.shape[0] == 1, "single-layer trunk"
    bp = x_prompt.shape[0]
    row = lambda a: a.reshape(1, -1).astype(F32)
    bf = lambda a: a.astype(BF16)
    lb = jnp.cumsum(jax.nn.softmax(hgrn_lb_logits.astype(F32), axis=0), axis=0)[0:1]
    head_id = jnp.arange(RW_WIDTH, dtype=jnp.int32) // RW_HEAD
    w = {
        'norm_mix': row(norm_mix[0]), 'w_in_hg': bf(w_in[0, :, :HG_COLS]), 'w_in_rw': bf(w_in[0, :, HG_COLS:]),
        'hgrn_norm': row(hgrn_norm[0]),
        'rw_vecs': tuple(row(a[0]) for a in (rw_mu, rw_w0, rw_a0, rw_k_k, rw_k_a, rw_r_k, rw_gn_w, rw_gn_b)),
        'rw_w_b': bf(rw_w_b[0]), 'rw_a_b': bf(rw_a_b[0]), 'rw_g_b': bf(rw_g_b[0]),
        'jmat': (head_id[:, None] == head_id[None, :]).astype(BF16),
        'w_out_hg': bf(w_out[0, :HG_WIDTH]), 'w_out_rw': bf(w_out[0, HG_WIDTH:]),
        'norm_cross': row(norm_cross[0]), 'w_cq': bf(w_cq[0]), 'w_co': bf(w_co[0]),
        'norm_ffn': row(norm_ffn[0]), 'w_ff1': bf(w_ff1[0]), 'w_ff3': bf(w_ff3[0]), 'w_ff2': bf(w_ff2[0]),
        'norm_final': row(norm_final),
    }
    mk, mv = _norm_matmul(mem_prompt.reshape(bp * N_MEM, D_MODEL), row(norm_mem[0]),
                          [bf(w_ck[0]), bf(w_cv[0])], "mem_kv")
    mk = mk.reshape(bp, N_MEM, D_MODEL)
    mv = mv.reshape(bp, N_MEM, D_MODEL)
    y_p, p_hg, p_rw, p_sh = _trunk(
        x_prompt, mk, mv, jnp.zeros((bp, HG_HEADS, HG_DIM, HG_DIM), F32),
        jnp.zeros((bp, RW_HEADS, RW_HEAD, RW_HEAD), F32), jnp.zeros((bp, 1, RW_COLS), F32), lb, w)
    bs = x_sample.shape[0]
    y_s, s_hg, s_rw, s_sh = _trunk(
        x_sample, cache_mem_k[0].reshape(bs, N_MEM, D_MODEL), cache_mem_v[0].reshape(bs, N_MEM, D_MODEL),
        state_hgrn[0], state_rwkv[0], state_rwkv_shift[0], lb, w)
    return (y_p, y_s, p_hg[None], p_rw[None], p_sh[None],
            mk.reshape(1, bp, N_MEM, X_HEADS, X_DIM), mv.reshape(1, bp, N_MEM, X_HEADS, X_DIM),
            s_hg[None], s_rw[None], s_sh[None])
```

```python
import functools

import jax
import jax.numpy as jnp
from jax import lax
from jax.experimental import pallas as pl
from jax.experimental.pallas import tpu as pltpu

F32 = jnp.float32
BF16 = jnp.bfloat16

D_MODEL = 1024
HG_WIDTH = 512
HG_HEADS = 4
HG_DIM = 128
HG_BLOCK = 16
HG_COLS = 4 * HG_WIDTH
RW_WIDTH = 512
RW_HEAD = 64
RW_HEADS = 8
RW_DECAY_LORA = 64
RW_A_LORA = 64
RW_GATE_LORA = 128
RW_COLS = 3 * RW_WIDTH + RW_DECAY_LORA + RW_A_LORA + RW_GATE_LORA
N_MEM = 256
X_HEADS = 4
X_DIM = D_MODEL // X_HEADS
D_FF = 2816
RMS_EPS = 1e-6
GN_EPS = 64e-5

TIME_CHUNK = 64
TIME_BLOCK = 256
RW_CHUNK_GROUP = 4
TOKEN_TILE = 512
FF_CHUNK = 1408
COL_CHUNK = 1024
VMEM_LIMIT_BYTES = 56 * 1024 * 1024

NN = (((1,), (0,)), ((), ()))
NT = (((1,), (1,)), ((), ()))
TN = (((0,), (0,)), ((), ()))


def _bdot(a, b, dims=NN):
    return lax.dot_general(a.astype(BF16), b.astype(BF16), dims, preferred_element_type=F32)


def _split3(x):
    hi = x.astype(BF16)
    r1 = x - hi.astype(F32)
    mid = r1.astype(BF16)
    lo = (r1 - mid.astype(F32)).astype(BF16)
    return hi, mid, lo


def _cumdot(mask3, x):
    return jnp.dot(mask3, jnp.concatenate(_split3(x), axis=0), preferred_element_type=F32)


def _head_sums(x):
    pair = 2 * RW_HEAD
    low = lax.broadcasted_iota(jnp.int32, (x.shape[0], pair), 1) < RW_HEAD
    out = []
    for c0 in range(0, x.shape[1], pair):
        xp = x[:, c0:c0 + pair]
        s_low = jnp.sum(jnp.where(low, xp, 0.0), axis=1, keepdims=True)
        s_high = jnp.sum(jnp.where(low, 0.0, xp), axis=1, keepdims=True)
        out.append(jnp.where(low, s_low, s_high))
    return jnp.concatenate(out, axis=1)


def _rmsnorm(x, g):
    return x * lax.rsqrt(jnp.mean(x * x, axis=-1, keepdims=True) + RMS_EPS) * g


def _softplus(x):
    return jnp.maximum(x, 0.0) + jnp.log(1.0 + jnp.exp(-jnp.abs(x)))


def _params(semantics):
    return pltpu.CompilerParams(dimension_semantics=semantics, vmem_limit_bytes=VMEM_LIMIT_BYTES)


def _const_spec(shape):
    zeros = (0,) * len(shape)
    return pl.BlockSpec(shape, lambda *_: zeros)


def _norm_matmul_kernel(x_ref, g_ref, *refs):
    n_out = len(refs) // 2
    w_refs, o_refs = refs[:n_out], refs[n_out:]
    n = _rmsnorm(x_ref[...], g_ref[...]).astype(BF16)
    for w_ref, o_ref in zip(w_refs, o_refs):
        cols = w_ref.shape[1]
        for c0 in range(0, cols, COL_CHUNK):
            c1 = min(c0 + COL_CHUNK, cols)
            o_ref[:, c0:c1] = jnp.dot(n, w_ref[:, c0:c1], preferred_element_type=F32).astype(o_ref.dtype)


def _norm_matmul(x, g, ws, name):
    rows, k = x.shape
    tm = min(TOKEN_TILE, rows)
    return pl.pallas_call(
        _norm_matmul_kernel,
        grid=(rows // tm,),
        in_specs=[pl.BlockSpec((tm, k), lambda i: (i, 0)), _const_spec((1, k))]
        + [_const_spec(w.shape) for w in ws],
        out_specs=[pl.BlockSpec((tm, w.shape[1]), lambda i: (i, 0)) for w in ws],
        out_shape=[jax.ShapeDtypeStruct((rows, w.shape[1]), F32) for w in ws],
        compiler_params=_params(("parallel",)),
        name=name,
    )(x, g, *ws)


def _out_proj_kernel(x_ref, a_ref, b_ref, wa_ref, wb_ref, o_ref):
    acc = jnp.dot(a_ref[...], wa_ref[...], preferred_element_type=F32)
    acc = acc + jnp.dot(b_ref[...], wb_ref[...], preferred_element_type=F32)
    o_ref[...] = x_ref[...] + acc


def _out_proj(x, a, b, wa, wb):
    rows, d = x.shape
    tm = min(TOKEN_TILE, rows)
    row_spec = lambda width: pl.BlockSpec((tm, width), lambda i: (i, 0))
    return pl.pallas_call(
        _out_proj_kernel,
        grid=(rows // tm,),
        in_specs=[row_spec(d), row_spec(a.shape[1]), row_spec(b.shape[1]),
                  _const_spec(wa.shape), _const_spec(wb.shape)],
        out_specs=row_spec(d),
        out_shape=jax.ShapeDtypeStruct((rows, d), F32),
        compiler_params=_params(("parallel",)),
        name="out_proj",
    )(x, a, b, wa, wb)


def _hgrn_kernel(p_ref, s0_ref, lb_ref, gn_ref, o_ref, sn_ref, st_scr, *, chunk):
    n_blk = chunk // HG_BLOCK
    W = HG_WIDTH
    heads = [slice(h * HG_DIM, (h + 1) * HG_DIM) for h in range(HG_HEADS)]
    hr = range(HG_HEADS)

    @pl.when(pl.program_id(1) == 0)
    def _():
        for h in hr:
            st_scr[h] = s0_ref[0, h].T

    ti = lax.broadcasted_iota(jnp.int32, (chunk, chunk), 0)
    tj = lax.broadcasted_iota(jnp.int32, (chunk, chunk), 1)
    diag_mask = (ti >= tj) & (ti // HG_BLOCK == tj // HG_BLOCK)
    ri = lax.broadcasted_iota(jnp.int32, (3 * chunk, 3 * chunk), 0)
    ci = lax.broadcasted_iota(jnp.int32, (3 * chunk, 3 * chunk), 1) % chunk
    kind, rt = ri // chunk, ri % chunk
    same = rt // HG_BLOCK == ci // HG_BLOCK
    as_f32 = lambda m: m.astype(F32)
    cum_mask = jnp.where(kind == 0, as_f32(rt >= ci),
                         jnp.where(kind == 1, as_f32((rt >= ci) & same),
                                   as_f32(ci // HG_BLOCK <= rt // HG_BLOCK))).astype(BF16)
    row_blk = lax.broadcasted_iota(jnp.int32, (chunk, 1), 0) // HG_BLOCK
    lb = lb_ref[...]
    gn = gn_ref[...]

    def one_chunk(i, carry):
        off = pl.multiple_of(i * chunk, chunk)
        p = p_ref[0, pl.ds(off, chunk), :]
        q, fpre, v, gate = p[:, :W], p[:, W:2 * W], p[:, 2 * W:3 * W], p[:, 3 * W:]
        sig = jax.nn.sigmoid(fpre)
        logf = jnp.log(lb + (1.0 - lb) * sig)
        k = (1.0 - lb) * (1.0 - sig)
        cums = _cumdot(cum_mask, logf)
        lc, lblk, lend = cums[:chunk], cums[chunk:2 * chunk], cums[2 * chunk:]
        llast = lc[chunk - 1:chunk, :]

        q_in = q * jnp.exp(lc)
        qg = q * jnp.exp(lblk)
        kg = k * jnp.exp(-lblk)
        kd = k * jnp.exp(lend - lc)
        k_end = k * jnp.exp(llast - lc)
        g_chunk = jnp.exp(llast)
        q_cross, k_cross = [], []
        for j in range(n_blk - 1):
            end_j = lend[j * HG_BLOCK:j * HG_BLOCK + 1, :]
            q_cross.append(jnp.where(row_blk > j, q * jnp.exp(jnp.minimum(lc - end_j, 0.0)), 0.0).astype(BF16))
            k_cross.append(jnp.where(row_blk == j, kd, 0.0).astype(BF16))

        att = [jnp.where(diag_mask, _bdot(qg[:, sl], kg[:, sl], NT), 0.0) for sl in heads]
        if n_blk > 1:
            att = [att[h] + _bdot(jnp.concatenate([qc[:, heads[h]] for qc in q_cross], axis=1),
                                  jnp.concatenate([kc[:, heads[h]] for kc in k_cross], axis=1), NT) for h in hr]
        st = [st_scr[h] for h in hr]
        o = [_bdot(q_in[:, heads[h]], st[h], NT) + _bdot(att[h], v[:, heads[h]]) for h in hr]
        for h in hr:
            st_scr[h] = st[h] * g_chunk[:, heads[h]] + _bdot(v[:, heads[h]], k_end[:, heads[h]], TN)
        for h in hr:
            on = o[h] * lax.rsqrt(jnp.mean(o[h] * o[h], axis=-1, keepdims=True) + RMS_EPS)
            o_ref[0, pl.ds(off, chunk), heads[h]] = (
                on * gn[:, heads[h]] * jax.nn.silu(gate[:, heads[h]])).astype(o_ref.dtype)
        return carry

    lax.fori_loop(0, p_ref.shape[1] // chunk, one_chunk, 0)

    @pl.when(pl.program_id(1) == pl.num_programs(1) - 1)
    def _():
        for h in hr:
            sn_ref[0, h] = st_scr[h].T


def _hgrn(p, s0, lb, gn):
    bsz, t_len, _ = p.shape
    chunk = min(TIME_CHUNK, t_len)
    t_blk = min(TIME_BLOCK, t_len)
    return pl.pallas_call(
        functools.partial(_hgrn_kernel, chunk=chunk),
        grid=(bsz, t_len // t_blk),
        in_specs=[pl.BlockSpec((1, t_blk, HG_COLS), lambda b, c: (b, c, 0)),
                  pl.BlockSpec((1, HG_HEADS, HG_DIM, HG_DIM), lambda b, c: (b, 0, 0, 0)),
                  _const_spec((1, HG_WIDTH)), _const_spec((1, HG_WIDTH))],
        out_specs=[pl.BlockSpec((1, t_blk, HG_WIDTH), lambda b, c: (b, c, 0)),
                   pl.BlockSpec((1, HG_HEADS, HG_DIM, HG_DIM), lambda b, c: (b, 0, 0, 0))],
        out_shape=[jax.ShapeDtypeStruct((bsz, t_len, HG_WIDTH), BF16),
                   jax.ShapeDtypeStruct((bsz, HG_HEADS, HG_DIM, HG_DIM), F32)],
        scratch_shapes=[pltpu.VMEM((HG_HEADS, HG_DIM, HG_DIM), F32)],
        compiler_params=_params(("parallel", "arbitrary")),
        name="hgrn",
    )(p, s0, lb, gn)


def _rwkv_kernel(p_ref, s0_ref, sh0_ref, mu_ref, w0_ref, wb_ref, a0_ref, ab_ref, gb_ref, kk_ref, ka_ref,
                 rk_ref, gnw_ref, gnb_ref, o_ref, sn_ref, shn_ref, s_scr, sh_scr, *, chunk):
    W, N, C = RW_WIDTH, RW_HEAD, chunk
    heads = [slice(h * N, (h + 1) * N) for h in range(RW_HEADS)]
    hr = range(RW_HEADS)

    @pl.when(pl.program_id(1) == 0)
    def _():
        s_scr[...] = s0_ref[0]
        sh_scr[...] = sh0_ref[0]

    n_chunks = p_ref.shape[1] // C
    group = RW_CHUNK_GROUP if n_chunks % RW_CHUNK_GROUP == 0 else 1
    R = group * C
    cr = range(group)
    units = [(ci, h) for ci in cr for h in hr]
    ur = range(len(units))
    row = lax.broadcasted_iota(jnp.int32, (R, 1), 0)
    ti = lax.broadcasted_iota(jnp.int32, (C, C), 0)
    tj = lax.broadcasted_iota(jnp.int32, (C, C), 1)
    incl = ti >= tj
    strict = ti > tj
    mi = lax.broadcasted_iota(jnp.int32, (R, 3 * R), 0)
    mj = lax.broadcasted_iota(jnp.int32, (R, 3 * R), 1) % R
    cum_mask = ((mi >= mj) & (mi // C == mj // C)).astype(BF16)

    def part(x, unit):
        ci, h = unit
        return x[ci * C:(ci + 1) * C, heads[h]]

    def one_group(i, carry):
        off = pl.multiple_of(i * R, R)
        p = p_ref[0, pl.ds(off, R), :]
        prev = jnp.where(row == 0, sh_scr[...], pltpu.roll(p, 1, 0))
        sh_scr[...] = p[R - 1:R, :]
        xs = p + (prev - p) * mu_ref[...]
        r, k, v = xs[:, :W], xs[:, W:2 * W], xs[:, 2 * W:3 * W]
        wd = xs[:, 3 * W:3 * W + RW_DECAY_LORA]
        ad = xs[:, 3 * W + RW_DECAY_LORA:3 * W + RW_DECAY_LORA + RW_A_LORA]
        gd = xs[:, 3 * W + RW_DECAY_LORA + RW_A_LORA:]

        w_log = -_softplus(-(w0_ref[...] + _bdot(jnp.tanh(wd), wb_ref[...]))) - 0.5
        ld = -jnp.exp(w_log)
        a_icl = jax.nn.sigmoid(a0_ref[...] + _bdot(ad, ab_ref[...]))
        gate = _bdot(jax.nn.sigmoid(gd), gb_ref[...])
        kk = k * kk_ref[...]
        kk = kk / jnp.maximum(jnp.sqrt(_head_sums(kk * kk)), 1e-12)
        k2 = k * (1.0 + (a_icl - 1.0) * ka_ref[...])
        av = -kk
        bv = kk * a_icl

        lc = _cumdot(cum_mask, ld)
        llast = [lc[ci * C + C - 1:ci * C + C, :] for ci in cr]
        to_end = jnp.concatenate([llast[ci] - lc[ci * C:(ci + 1) * C] for ci in cr], axis=0)
        e_in = jnp.exp(lc)
        e_inv = jnp.exp(-lc)
        e_end = jnp.exp(to_end)
        a_t = av * jnp.exp(lc - ld)
        r_t = r * e_in
        b_t = bv * e_inv
        k_t = k2 * e_inv
        b_h = bv * e_end
        k_h = k2 * e_end
        g_chunk = [jnp.exp(llast[ci]) for ci in cr]

        g = [_bdot(jnp.concatenate([part(a_t, u), part(r_t, u)], axis=0),
                   jnp.concatenate([part(b_t, u), part(k_t, u)], axis=0), NT) for u in units]
        l_pow = [jnp.where(strict, g[n][:C, :C], 0.0) for n in ur]
        l_ak = [jnp.where(strict, g[n][:C, C:], 0.0) for n in ur]
        a_rb = [jnp.where(incl, g[n][C:, :C], 0.0) for n in ur]
        a_rk = [jnp.where(incl, g[n][C:, C:], 0.0) for n in ur]
        vh = [part(v, u) for u in units]
        au = [jnp.concatenate([part(a_t, units[n]), _bdot(l_ak[n], vh[n])], axis=1) for n in ur]
        span = 1
        while True:
            au = [au[n] + _bdot(l_pow[n], au[n]) for n in ur]
            span *= 2
            if span >= C:
                break
            l_pow = [_bdot(l, l) for l in l_pow]
        mn = [_bdot(au[n], part(b_h, units[n]), TN) for n in ur]
        nt = [mn[n][N:] + _bdot(vh[n], part(k_h, units[n]), TN) for n in ur]
        ry = [_bdot(a_rb[n], au[n]) for n in ur]
        y0 = [ry[n][:, N:] + _bdot(a_rk[n], vh[n]) for n in ur]
        r_hat = [part(r_t, units[n]) + ry[n][:, :N] for n in ur]
        s_cur = [s_scr[h] for h in hr]
        ys = []
        for ci in cr:
            base = ci * RW_HEADS
            ys.append([_bdot(r_hat[base + h], s_cur[h], NT) + y0[base + h] for h in hr])
            s_cur = [s_cur[h] * g_chunk[ci][:, heads[h]] + _bdot(s_cur[h], mn[base + h][:N]) + nt[base + h]
                     for h in hr]
        for h in hr:
            s_scr[h] = s_cur[h]

        y = jnp.concatenate([jnp.concatenate(ys[ci], axis=1) for ci in cr], axis=0)
        mean = _head_sums(y) * (1.0 / N)
        yc = y - mean
        var = _head_sums(yc * yc) * (1.0 / N)
        yn = yc * lax.rsqrt(var + GN_EPS) * gnw_ref[...] + gnb_ref[...]
        bonus = _head_sums(r * k2 * rk_ref[...]) * v
        o_ref[0, pl.ds(off, R), :] = ((yn + bonus) * gate).astype(o_ref.dtype)
        return carry

    lax.fori_loop(0, n_chunks // group, one_group, 0)

    @pl.when(pl.program_id(1) == pl.num_programs(1) - 1)
    def _():
        sn_ref[0] = s_scr[...]
        shn_ref[0] = sh_scr[...]


def _rwkv(p, s0, sh0, vecs, wb, ab, gb):
    bsz, t_len, _ = p.shape
    chunk = min(TIME_CHUNK, t_len)
    t_blk = min(TIME_BLOCK, t_len)
    mu, w0, a0, kk, ka, rk, gnw, gnb = vecs
    state_spec = pl.BlockSpec((1, RW_HEADS, RW_HEAD, RW_HEAD), lambda b, c: (b, 0, 0, 0))
    shift_spec = pl.BlockSpec((1, 1, RW_COLS), lambda b, c: (b, 0, 0))
    return pl.pallas_call(
        functools.partial(_rwkv_kernel, chunk=chunk),
        grid=(bsz, t_len // t_blk),
        in_specs=[pl.BlockSpec((1, t_blk, RW_COLS), lambda b, c: (b, c, 0)), state_spec, shift_spec,
                  _const_spec(mu.shape), _const_spec(w0.shape), _const_spec(wb.shape), _const_spec(a0.shape),
                  _const_spec(ab.shape), _const_spec(gb.shape), _const_spec(kk.shape), _const_spec(ka.shape),
                  _const_spec(rk.shape), _const_spec(gnw.shape), _const_spec(gnb.shape)],
        out_specs=[pl.BlockSpec((1, t_blk, RW_WIDTH), lambda b, c: (b, c, 0)), state_spec, shift_spec],
        out_shape=[jax.ShapeDtypeStruct((bsz, t_len, RW_WIDTH), BF16),
                   jax.ShapeDtypeStruct((bsz, RW_HEADS, RW_HEAD, RW_HEAD), F32),
                   jax.ShapeDtypeStruct((bsz, 1, RW_COLS), F32)],
        scratch_shapes=[pltpu.VMEM((RW_HEADS, RW_HEAD, RW_HEAD), F32), pltpu.VMEM((1, RW_COLS), F32)],
        compiler_params=_params(("parallel", "arbitrary")),
        name="rwkv",
    )(p, s0, sh0, mu, w0, wb, a0, ab, gb, kk, ka, rk, gnw, gnb)


def _cross_kernel(x_ref, k_ref, v_ref, g_ref, wq_ref, wo_ref, o_ref):
    x = x_ref[0]
    q = jnp.dot(_rmsnorm(x, g_ref[...]).astype(BF16), wq_ref[...], preferred_element_type=F32)
    heads = []
    for h in range(X_HEADS):
        sl = slice(h * X_DIM, (h + 1) * X_DIM)
        s = _bdot(q[:, sl], k_ref[0, :, sl], NT) * (X_DIM ** -0.5)
        e = jnp.exp(s - jnp.max(s, axis=-1, keepdims=True))
        pr = e / jnp.sum(e, axis=-1, keepdims=True)
        heads.append(_bdot(pr, v_ref[0, :, sl]).astype(BF16))
    o_ref[0] = x + jnp.dot(jnp.concatenate(heads, axis=1), wo_ref[...], preferred_element_type=F32)


def _cross(x, mk, mv, g, wq, wo):
    bsz, t_len, d = x.shape
    tq = min(TOKEN_TILE, t_len)
    x_spec = pl.BlockSpec((1, tq, d), lambda b, t: (b, t, 0))
    mem_spec = pl.BlockSpec((1, N_MEM, d), lambda b, t: (b, 0, 0))
    return pl.pallas_call(
        _cross_kernel,
        grid=(bsz, t_len // tq),
        in_specs=[x_spec, mem_spec, mem_spec, _const_spec((1, d)), _const_spec(wq.shape), _const_spec(wo.shape)],
        out_specs=x_spec,
        out_shape=jax.ShapeDtypeStruct(x.shape, F32),
        compiler_params=_params(("parallel", "parallel")),
        name="cross",
    )(x, mk, mv, g, wq, wo)


def _ffn_kernel(x_ref, g_ref, w1_ref, w3_ref, w2_ref, gf_ref, o_ref):
    x = x_ref[...]
    n = _rmsnorm(x, g_ref[...]).astype(BF16)
    acc = x
    for c0 in range(0, D_FF, FF_CHUNK):
        c1 = c0 + FF_CHUNK
        up = jnp.dot(n, w1_ref[:, c0:c1], preferred_element_type=F32)
        lin = jnp.dot(n, w3_ref[:, c0:c1], preferred_element_type=F32)
        hid = (jax.nn.silu(up) * lin).astype(BF16)
        acc = acc + jnp.dot(hid, w2_ref[c0:c1, :], preferred_element_type=F32)
    o_ref[...] = _rmsnorm(acc, gf_ref[...])


def _ffn(x, g, w1, w3, w2, gf):
    rows, d = x.shape
    tm = min(TOKEN_TILE, rows)
    row_spec = pl.BlockSpec((tm, d), lambda i: (i, 0))
    return pl.pallas_call(
        _ffn_kernel,
        grid=(rows // tm,),
        in_specs=[row_spec, _const_spec((1, d)), _const_spec(w1.shape), _const_spec(w3.shape),
                  _const_spec(w2.shape), _const_spec((1, d))],
        out_specs=row_spec,
        out_shape=jax.ShapeDtypeStruct((rows, d), F32),
        compiler_params=_params(("parallel",)),
        name="ffn",
    )(x, g, w1, w3, w2, gf)


def _trunk(x, mk, mv, hg0, rw0, sh0, lb, w):
    bsz, t_len, d = x.shape
    x2d = x.reshape(bsz * t_len, d)
    p_hg, p_rw = _norm_matmul(x2d, w['norm_mix'], [w['w_in_hg'], w['w_in_rw']], "in_proj")
    hg_o, hg_s = _hgrn(p_hg.reshape(bsz, t_len, HG_COLS), hg0, lb, w['hgrn_norm'])
    rw_o, rw_s, rw_sh = _rwkv(p_rw.reshape(bsz, t_len, RW_COLS), rw0, sh0, w['rw_vecs'],
                              w['rw_w_b'], w['rw_a_b'], w['rw_g_b'])
    x2d = _out_proj(x2d, hg_o.reshape(bsz * t_len, HG_WIDTH), rw_o.reshape(bsz * t_len, RW_WIDTH),
                    w['w_out_hg'], w['w_out_rw'])
    x3d = _cross(x2d.reshape(bsz, t_len, d), mk, mv, w['norm_cross'], w['w_cq'], w['w_co'])
    y = _ffn(x3d.reshape(bsz * t_len, d), w['norm_ffn'], w['w_ff1'], w['w_ff3'], w['w_ff2'], w['norm_final'])
    return y.reshape(bsz, t_len, d), hg_s, rw_s, rw_sh


def kernel(x_prompt, mem_prompt, x_sample, cache_mem_k, cache_mem_v, state_hgrn, state_rwkv, state_rwkv_shift, hgrn_lb_logits, norm_mix, w_in, hgrn_norm, rw_mu, rw_w0, rw_w_b, rw_a0, rw_a_b, rw_g_b, rw_k_k, rw_k_a, rw_r_k, rw_gn_w, rw_gn_b, w_out, norm_cross, norm_mem, w_cq, w_ck, w_cv, w_co, norm_ffn, w_ff1, w_ff3, w_ff2, norm_final):
    assert w_in.shape[0] == 1, "single-layer trunk"
    bp = x_prompt.shape[0]
    row = lambda a: a.reshape(1, -1).astype(F32)
    bf = lambda a: a.astype(BF16)
    lb = jnp.cumsum(jax.nn.softmax(hgrn_lb_logits.astype(F32), axis=0), axis=0)[0:1]
    w = {
        'norm_mix': row(norm_mix[0]), 'w_in_hg': bf(w_in[0, :, :HG_COLS]), 'w_in_rw': bf(w_in[0, :, HG_COLS:]),
        'hgrn_norm': row(hgrn_norm[0]),
        'rw_vecs': tuple(row(a[0]) for a in (rw_mu, rw_w0, rw_a0, rw_k_k, rw_k_a, rw_r_k, rw_gn_w, rw_gn_b)),
        'rw_w_b': bf(rw_w_b[0]), 'rw_a_b': bf(rw_a_b[0]), 'rw_g_b': bf(rw_g_b[0]),
        'w_out_hg': bf(w_out[0, :HG_WIDTH]), 'w_out_rw': bf(w_out[0, HG_WIDTH:]),
        'norm_cross': row(norm_cross[0]), 'w_cq': bf(w_cq[0]), 'w_co': bf(w_co[0]),
        'norm_ffn': row(norm_ffn[0]), 'w_ff1': bf(w_ff1[0]), 'w_ff3': bf(w_ff3[0]), 'w_ff2': bf(w_ff2[0]),
        'norm_final': row(norm_final),
    }
    mk, mv = _norm_matmul(mem_prompt.reshape(bp * N_MEM, D_MODEL), row(norm_mem[0]),
                          [bf(w_ck[0]), bf(w_cv[0])], "mem_kv")
    mk = mk.reshape(bp, N_MEM, D_MODEL)
    mv = mv.reshape(bp, N_MEM, D_MODEL)
    y_p, p_hg, p_rw, p_sh = _trunk(
        x_prompt, mk, mv, jnp.zeros((bp, HG_HEADS, HG_DIM, HG_DIM), F32),
        jnp.zeros((bp, RW_HEADS, RW_HEAD, RW_HEAD), F32), jnp.zeros((bp, 1, RW_COLS), F32), lb, w)
    bs = x_sample.shape[0]
    y_s, s_hg, s_rw, s_sh = _trunk(
        x_sample, cache_mem_k[0].reshape(bs, N_MEM, D_MODEL), cache_mem_v[0].reshape(bs, N_MEM, D_MODEL),
        state_hgrn[0], state_rwkv[0], state_rwkv_shift[0], lb, w)
    return (y_p, y_s, p_hg[None], p_rw[None], p_sh[None],
            mk.reshape(1, bp, N_MEM, X_HEADS, X_DIM), mv.reshape(1, bp, N_MEM, X_HEADS, X_DIM),
            s_hg[None], s_rw[None], s_sh[None])
```

```python
import functools

import jax
import jax.numpy as jnp
from jax import lax
from jax.experimental import pallas as pl
from jax.experimental.pallas import tpu as pltpu

F32 = jnp.float32
BF16 = jnp.bfloat16

D_MODEL = 1024
HG_WIDTH = 512
HG_HEADS = 4
HG_DIM = 128
HG_BLOCK = 16
HG_COLS = 4 * HG_WIDTH
RW_WIDTH = 512
RW_HEAD = 64
RW_HEADS = 8
RW_DECAY_LORA = 64
RW_A_LORA = 64
RW_GATE_LORA = 128
RW_COLS = 3 * RW_WIDTH + RW_DECAY_LORA + RW_A_LORA + RW_GATE_LORA
N_MEM = 256
X_HEADS = 4
X_DIM = D_MODEL // X_HEADS
D_FF = 2816
RMS_EPS = 1e-6
GN_EPS = 64e-5

TIME_CHUNK = 64
TIME_BLOCK = 512
HG_CHUNK_GROUP = 4
RW_CHUNK_GROUP = 4
TOKEN_TILE = 512
FF_CHUNK = 1408
COL_CHUNK = 1024
VMEM_LIMIT_BYTES = 56 * 1024 * 1024

NN = (((1,), (0,)), ((), ()))
NT = (((1,), (1,)), ((), ()))
TN = (((0,), (0,)), ((), ()))


def _bdot(a, b, dims=NN):
    return lax.dot_general(a.astype(BF16), b.astype(BF16), dims, preferred_element_type=F32)


def _split3(x):
    hi = x.astype(BF16)
    r1 = x - hi.astype(F32)
    mid = r1.astype(BF16)
    lo = (r1 - mid.astype(F32)).astype(BF16)
    return hi, mid, lo


def _cumdot(mask3, x):
    return jnp.dot(mask3, jnp.concatenate(_split3(x), axis=0), preferred_element_type=F32)


def _head_sums(x):
    pair = 2 * RW_HEAD
    low = lax.broadcasted_iota(jnp.int32, (x.shape[0], pair), 1) < RW_HEAD
    out = []
    for c0 in range(0, x.shape[1], pair):
        xp = x[:, c0:c0 + pair]
        s_low = jnp.sum(jnp.where(low, xp, 0.0), axis=1, keepdims=True)
        s_high = jnp.sum(jnp.where(low, 0.0, xp), axis=1, keepdims=True)
        out.append(jnp.where(low, s_low, s_high))
    return jnp.concatenate(out, axis=1)


def _rmsnorm(x, g):
    return x * lax.rsqrt(jnp.mean(x * x, axis=-1, keepdims=True) + RMS_EPS) * g


def _softplus(x):
    return jnp.maximum(x, 0.0) + jnp.log(1.0 + jnp.exp(-jnp.abs(x)))


def _params(semantics):
    return pltpu.CompilerParams(dimension_semantics=semantics, vmem_limit_bytes=VMEM_LIMIT_BYTES)


def _const_spec(shape):
    zeros = (0,) * len(shape)
    return pl.BlockSpec(shape, lambda *_: zeros)


def _norm_matmul_kernel(x_ref, g_ref, *refs):
    n_out = len(refs) // 2
    w_refs, o_refs = refs[:n_out], refs[n_out:]
    n = _rmsnorm(x_ref[...], g_ref[...]).astype(BF16)
    for w_ref, o_ref in zip(w_refs, o_refs):
        cols = w_ref.shape[1]
        for c0 in range(0, cols, COL_CHUNK):
            c1 = min(c0 + COL_CHUNK, cols)
            o_ref[:, c0:c1] = jnp.dot(n, w_ref[:, c0:c1], preferred_element_type=F32).astype(o_ref.dtype)


def _norm_matmul(x, g, ws, name):
    rows, k = x.shape
    tm = min(TOKEN_TILE, rows)
    return pl.pallas_call(
        _norm_matmul_kernel,
        grid=(rows // tm,),
        in_specs=[pl.BlockSpec((tm, k), lambda i: (i, 0)), _const_spec((1, k))]
        + [_const_spec(w.shape) for w in ws],
        out_specs=[pl.BlockSpec((tm, w.shape[1]), lambda i: (i, 0)) for w in ws],
        out_shape=[jax.ShapeDtypeStruct((rows, w.shape[1]), F32) for w in ws],
        compiler_params=_params(("parallel",)),
        name=name,
    )(x, g, *ws)


def _hgrn_kernel(p_ref, s0_ref, lb_ref, gn_ref, o_ref, sn_ref, st_scr, *, chunk):
    n_blk = chunk // HG_BLOCK
    W = HG_WIDTH
    heads = [slice(h * HG_DIM, (h + 1) * HG_DIM) for h in range(HG_HEADS)]
    hr = range(HG_HEADS)

    @pl.when(pl.program_id(1) == 0)
    def _():
        for h in hr:
            st_scr[h] = s0_ref[0, h].T

    ti = lax.broadcasted_iota(jnp.int32, (chunk, chunk), 0)
    tj = lax.broadcasted_iota(jnp.int32, (chunk, chunk), 1)
    diag_mask = (ti >= tj) & (ti // HG_BLOCK == tj // HG_BLOCK)
    ri = lax.broadcasted_iota(jnp.int32, (3 * chunk, 3 * chunk), 0)
    ci = lax.broadcasted_iota(jnp.int32, (3 * chunk, 3 * chunk), 1) % chunk
    kind, rt = ri // chunk, ri % chunk
    same = rt // HG_BLOCK == ci // HG_BLOCK
    as_f32 = lambda m: m.astype(F32)
    cum_mask = jnp.where(kind == 0, as_f32(rt >= ci),
                         jnp.where(kind == 1, as_f32((rt >= ci) & same),
                                   as_f32(ci // HG_BLOCK <= rt // HG_BLOCK))).astype(BF16)
    n_chunks = p_ref.shape[1] // chunk
    group = HG_CHUNK_GROUP if n_chunks % HG_CHUNK_GROUP == 0 else 1
    R = group * chunk
    cr = range(group)
    units = [(c, h) for c in cr for h in hr]
    ur = range(len(units))
    row_blk = (lax.broadcasted_iota(jnp.int32, (R, 1), 0) % chunk) // HG_BLOCK
    lb = lb_ref[...]
    gn = gn_ref[...]

    def rows(x, c):
        return x[c * chunk:(c + 1) * chunk]

    def part(x, unit):
        c, h = unit
        return x[c * chunk:(c + 1) * chunk, heads[h]]

    def one_group(i, carry):
        off = pl.multiple_of(i * R, R)
        p = p_ref[0, pl.ds(off, R), :]
        q, fpre, v, gate = p[:, :W], p[:, W:2 * W], p[:, 2 * W:3 * W], p[:, 3 * W:]
        sig = jax.nn.sigmoid(fpre)
        logf = jnp.log(lb + (1.0 - lb) * sig)
        k = (1.0 - lb) * (1.0 - sig)
        cums = [_cumdot(cum_mask, rows(logf, c)) for c in cr]
        stack = lambda pieces: jnp.concatenate(pieces, axis=0) if len(pieces) > 1 else pieces[0]
        lc = stack([cums[c][:chunk] for c in cr])
        lblk = stack([cums[c][chunk:2 * chunk] for c in cr])
        lend = stack([cums[c][2 * chunk:] for c in cr])
        llast = [cums[c][chunk - 1:chunk, :] for c in cr]
        to_end = stack([llast[c] - cums[c][:chunk] for c in cr])

        q_in = q * jnp.exp(lc)
        qg = q * jnp.exp(lblk)
        kg = k * jnp.exp(-lblk)
        kd = k * jnp.exp(lend - lc)
        k_end = k * jnp.exp(to_end)
        g_chunk = [jnp.exp(llast[c]) for c in cr]
        q_cross, k_cross = [], []
        for j in range(n_blk - 1):
            end_j = stack([jnp.broadcast_to(cums[c][2 * chunk + j * HG_BLOCK:2 * chunk + j * HG_BLOCK + 1, :],
                                            (chunk, W)) for c in cr])
            q_cross.append(jnp.where(row_blk > j, q * jnp.exp(jnp.minimum(lc - end_j, 0.0)), 0.0).astype(BF16))
            k_cross.append(jnp.where(row_blk == j, kd, 0.0).astype(BF16))

        att = [jnp.where(diag_mask, _bdot(part(qg, u), part(kg, u), NT), 0.0) for u in units]
        if n_blk > 1:
            att = [att[n] + _bdot(jnp.concatenate([part(qc, units[n]) for qc in q_cross], axis=1),
                                  jnp.concatenate([part(kc, units[n]) for kc in k_cross], axis=1), NT) for n in ur]
        o_intra = [_bdot(att[n], part(v, units[n])) for n in ur]
        kv = [_bdot(part(v, u), part(k_end, u), TN) for u in units]
        st = [st_scr[h] for h in hr]
        for c in cr:
            base = c * HG_HEADS
            o = [_bdot(part(q_in, (c, h)), st[h], NT) + o_intra[base + h] for h in hr]
            st = [st[h] * g_chunk[c][:, heads[h]] + kv[base + h] for h in hr]
            for h in hr:
                on = o[h] * lax.rsqrt(jnp.mean(o[h] * o[h], axis=-1, keepdims=True) + RMS_EPS)
                o_ref[0, pl.ds(off + c * chunk, chunk), heads[h]] = (
                    on * gn[:, heads[h]] * jax.nn.silu(part(gate, (c, h)))).astype(o_ref.dtype)
        for h in hr:
            st_scr[h] = st[h]
        return carry

    lax.fori_loop(0, n_chunks // group, one_group, 0)

    @pl.when(pl.program_id(1) == pl.num_programs(1) - 1)
    def _():
        for h in hr:
            sn_ref[0, h] = st_scr[h].T


def _hgrn(p, s0, lb, gn):
    bsz, t_len, _ = p.shape
    chunk = min(TIME_CHUNK, t_len)
    t_blk = min(TIME_BLOCK, t_len)
    return pl.pallas_call(
        functools.partial(_hgrn_kernel, chunk=chunk),
        grid=(bsz, t_len // t_blk),
        in_specs=[pl.BlockSpec((1, t_blk, HG_COLS), lambda b, c: (b, c, 0)),
                  pl.BlockSpec((1, HG_HEADS, HG_DIM, HG_DIM), lambda b, c: (b, 0, 0, 0)),
                  _const_spec((1, HG_WIDTH)), _const_spec((1, HG_WIDTH))],
        out_specs=[pl.BlockSpec((1, t_blk, HG_WIDTH), lambda b, c: (b, c, 0)),
                   pl.BlockSpec((1, HG_HEADS, HG_DIM, HG_DIM), lambda b, c: (b, 0, 0, 0))],
        out_shape=[jax.ShapeDtypeStruct((bsz, t_len, HG_WIDTH), BF16),
                   jax.ShapeDtypeStruct((bsz, HG_HEADS, HG_DIM, HG_DIM), F32)],
        scratch_shapes=[pltpu.VMEM((HG_HEADS, HG_DIM, HG_DIM), F32)],
        compiler_params=_params(("parallel", "arbitrary")),
        name="hgrn",
    )(p, s0, lb, gn)


def _rwkv_kernel(p_ref, s0_ref, sh0_ref, mu_ref, w0_ref, wb_ref, a0_ref, ab_ref, gb_ref, kk_ref, ka_ref,
                 rk_ref, gnw_ref, gnb_ref, o_ref, sn_ref, shn_ref, s_scr, sh_scr, *, chunk):
    W, N, C = RW_WIDTH, RW_HEAD, chunk
    heads = [slice(h * N, (h + 1) * N) for h in range(RW_HEADS)]
    hr = range(RW_HEADS)

    @pl.when(pl.program_id(1) == 0)
    def _():
        s_scr[...] = s0_ref[0]
        sh_scr[...] = sh0_ref[0]

    n_chunks = p_ref.shape[1] // C
    group = RW_CHUNK_GROUP if n_chunks % RW_CHUNK_GROUP == 0 else 1
    R = group * C
    cr = range(group)
    units = [(ci, h) for ci in cr for h in hr]
    ur = range(len(units))
    row = lax.broadcasted_iota(jnp.int32, (R, 1), 0)
    ti = lax.broadcasted_iota(jnp.int32, (C, C), 0)
    tj = lax.broadcasted_iota(jnp.int32, (C, C), 1)
    incl = ti >= tj
    strict = ti > tj
    mi = lax.broadcasted_iota(jnp.int32, (R, 3 * R), 0)
    mj = lax.broadcasted_iota(jnp.int32, (R, 3 * R), 1) % R
    cum_mask = ((mi >= mj) & (mi // C == mj // C)).astype(BF16)

    def part(x, unit):
        ci, h = unit
        return x[ci * C:(ci + 1) * C, heads[h]]

    def one_group(i, carry):
        off = pl.multiple_of(i * R, R)
        p = p_ref[0, pl.ds(off, R), :]
        prev = jnp.where(row == 0, sh_scr[...], pltpu.roll(p, 1, 0))
        sh_scr[...] = p[R - 1:R, :]
        xs = p + (prev - p) * mu_ref[...]
        r, k, v = xs[:, :W], xs[:, W:2 * W], xs[:, 2 * W:3 * W]
        wd = xs[:, 3 * W:3 * W + RW_DECAY_LORA]
        ad = xs[:, 3 * W + RW_DECAY_LORA:3 * W + RW_DECAY_LORA + RW_A_LORA]
        gd = xs[:, 3 * W + RW_DECAY_LORA + RW_A_LORA:]

        w_log = -_softplus(-(w0_ref[...] + _bdot(jnp.tanh(wd), wb_ref[...]))) - 0.5
        ld = -jnp.exp(w_log)
        a_icl = jax.nn.sigmoid(a0_ref[...] + _bdot(ad, ab_ref[...]))
        gate = _bdot(jax.nn.sigmoid(gd), gb_ref[...])
        kk = k * kk_ref[...]
        kk = kk / jnp.maximum(jnp.sqrt(_head_sums(kk * kk)), 1e-12)
        k2 = k * (1.0 + (a_icl - 1.0) * ka_ref[...])
        av = -kk
        bv = kk * a_icl

        lc = _cumdot(cum_mask, ld)
        llast = [lc[ci * C + C - 1:ci * C + C, :] for ci in cr]
        to_end = jnp.concatenate([llast[ci] - lc[ci * C:(ci + 1) * C] for ci in cr], axis=0)
        e_in = jnp.exp(lc)
        e_inv = jnp.exp(-lc)
        e_end = jnp.exp(to_end)
        a_t = av * jnp.exp(lc - ld)
        r_t = r * e_in
        b_t = bv * e_inv
        k_t = k2 * e_inv
        b_h = bv * e_end
        k_h = k2 * e_end
        g_chunk = [jnp.exp(llast[ci]) for ci in cr]

        g = [_bdot(jnp.concatenate([part(a_t, u), part(r_t, u)], axis=0),
                   jnp.concatenate([part(b_t, u), part(k_t, u)], axis=0), NT) for u in units]
        l_pow = [jnp.where(strict, g[n][:C, :C], 0.0) for n in ur]
        l_ak = [jnp.where(strict, g[n][:C, C:], 0.0) for n in ur]
        a_rb = [jnp.where(incl, g[n][C:, :C], 0.0) for n in ur]
        a_rk = [jnp.where(incl, g[n][C:, C:], 0.0) for n in ur]
        vh = [part(v, u) for u in units]
        au = [jnp.concatenate([part(a_t, units[n]), _bdot(l_ak[n], vh[n])], axis=1) for n in ur]
        span = 1
        while True:
            au = [au[n] + _bdot(l_pow[n], au[n]) for n in ur]
            span *= 2
            if span >= C:
                break
            l_pow = [_bdot(l, l) for l in l_pow]
        mn = [_bdot(au[n], part(b_h, units[n]), TN) for n in ur]
        nt = [mn[n][N:] + _bdot(vh[n], part(k_h, units[n]), TN) for n in ur]
        ry = [_bdot(a_rb[n], au[n]) for n in ur]
        y0 = [ry[n][:, N:] + _bdot(a_rk[n], vh[n]) for n in ur]
        r_hat = [part(r_t, units[n]) + ry[n][:, :N] for n in ur]
        s_cur = [s_scr[h] for h in hr]
        ys = []
        for ci in cr:
            base = ci * RW_HEADS
            ys.append([_bdot(r_hat[base + h], s_cur[h], NT) + y0[base + h] for h in hr])
            s_cur = [s_cur[h] * g_chunk[ci][:, heads[h]] + _bdot(s_cur[h], mn[base + h][:N]) + nt[base + h]
                     for h in hr]
        for h in hr:
            s_scr[h] = s_cur[h]

        y = jnp.concatenate([jnp.concatenate(ys[ci], axis=1) for ci in cr], axis=0)
        mean = _head_sums(y) * (1.0 / N)
        yc = y - mean
        var = _head_sums(yc * yc) * (1.0 / N)
        yn = yc * lax.rsqrt(var + GN_EPS) * gnw_ref[...] + gnb_ref[...]
        bonus = _head_sums(r * k2 * rk_ref[...]) * v
        o_ref[0, pl.ds(off, R), :] = ((yn + bonus) * gate).astype(o_ref.dtype)
        return carry

    lax.fori_loop(0, n_chunks // group, one_group, 0)

    @pl.when(pl.program_id(1) == pl.num_programs(1) - 1)
    def _():
        sn_ref[0] = s_scr[...]
        shn_ref[0] = sh_scr[...]


def _rwkv(p, s0, sh0, vecs, wb, ab, gb):
    bsz, t_len, _ = p.shape
    chunk = min(TIME_CHUNK, t_len)
    t_blk = min(TIME_BLOCK, t_len)
    mu, w0, a0, kk, ka, rk, gnw, gnb = vecs
    state_spec = pl.BlockSpec((1, RW_HEADS, RW_HEAD, RW_HEAD), lambda b, c: (b, 0, 0, 0))
    shift_spec = pl.BlockSpec((1, 1, RW_COLS), lambda b, c: (b, 0, 0))
    return pl.pallas_call(
        functools.partial(_rwkv_kernel, chunk=chunk),
        grid=(bsz, t_len // t_blk),
        in_specs=[pl.BlockSpec((1, t_blk, RW_COLS), lambda b, c: (b, c, 0)), state_spec, shift_spec,
                  _const_spec(mu.shape), _const_spec(w0.shape), _const_spec(wb.shape), _const_spec(a0.shape),
                  _const_spec(ab.shape), _const_spec(gb.shape), _const_spec(kk.shape), _const_spec(ka.shape),
                  _const_spec(rk.shape), _const_spec(gnw.shape), _const_spec(gnb.shape)],
        out_specs=[pl.BlockSpec((1, t_blk, RW_WIDTH), lambda b, c: (b, c, 0)), state_spec, shift_spec],
        out_shape=[jax.ShapeDtypeStruct((bsz, t_len, RW_WIDTH), BF16),
                   jax.ShapeDtypeStruct((bsz, RW_HEADS, RW_HEAD, RW_HEAD), F32),
                   jax.ShapeDtypeStruct((bsz, 1, RW_COLS), F32)],
        scratch_shapes=[pltpu.VMEM((RW_HEADS, RW_HEAD, RW_HEAD), F32), pltpu.VMEM((1, RW_COLS), F32)],
        compiler_params=_params(("parallel", "arbitrary")),
        name="rwkv",
    )(p, s0, sh0, mu, w0, wb, a0, ab, gb, kk, ka, rk, gnw, gnb)


def _cross_kernel(x_ref, a_ref, b_ref, k_ref, v_ref, wa_ref, wb_ref, g_ref, wq_ref, wo_ref, o_ref):
    x = x_ref[0] + jnp.dot(a_ref[0], wa_ref[...], preferred_element_type=F32)
    x = x + jnp.dot(b_ref[0], wb_ref[...], preferred_element_type=F32)
    q = jnp.dot(_rmsnorm(x, g_ref[...]).astype(BF16), wq_ref[...], preferred_element_type=F32)
    heads = []
    for h in range(X_HEADS):
        sl = slice(h * X_DIM, (h + 1) * X_DIM)
        s = _bdot(q[:, sl], k_ref[0, :, sl], NT) * (X_DIM ** -0.5)
        e = jnp.exp(s - jnp.max(s, axis=-1, keepdims=True))
        pr = e / jnp.sum(e, axis=-1, keepdims=True)
        heads.append(_bdot(pr, v_ref[0, :, sl]).astype(BF16))
    o_ref[0] = x + jnp.dot(jnp.concatenate(heads, axis=1), wo_ref[...], preferred_element_type=F32)


def _cross(x, a, b, mk, mv, wa, wb, g, wq, wo):
    bsz, t_len, d = x.shape
    tq = min(TOKEN_TILE, t_len)
    tok_spec = lambda width: pl.BlockSpec((1, tq, width), lambda i, t: (i, t, 0))
    mem_spec = pl.BlockSpec((1, N_MEM, d), lambda i, t: (i, 0, 0))
    return pl.pallas_call(
        _cross_kernel,
        grid=(bsz, t_len // tq),
        in_specs=[tok_spec(d), tok_spec(a.shape[2]), tok_spec(b.shape[2]), mem_spec, mem_spec,
                  _const_spec(wa.shape), _const_spec(wb.shape), _const_spec((1, d)),
                  _const_spec(wq.shape), _const_spec(wo.shape)],
        out_specs=tok_spec(d),
        out_shape=jax.ShapeDtypeStruct(x.shape, F32),
        compiler_params=_params(("parallel", "parallel")),
        name="cross",
    )(x, a, b, mk, mv, wa, wb, g, wq, wo)


def _ffn_kernel(x_ref, g_ref, w1_ref, w3_ref, w2_ref, gf_ref, o_ref):
    x = x_ref[...]
    n = _rmsnorm(x, g_ref[...]).astype(BF16)
    acc = x
    for c0 in range(0, D_FF, FF_CHUNK):
        c1 = c0 + FF_CHUNK
        up = jnp.dot(n, w1_ref[:, c0:c1], preferred_element_type=F32)
        lin = jnp.dot(n, w3_ref[:, c0:c1], preferred_element_type=F32)
        hid = (jax.nn.silu(up) * lin).astype(BF16)
        acc = acc + jnp.dot(hid, w2_ref[c0:c1, :], preferred_element_type=F32)
    o_ref[...] = _rmsnorm(acc, gf_ref[...])


def _ffn(x, g, w1, w3, w2, gf):
    rows, d = x.shape
    tm = min(TOKEN_TILE, rows)
    row_spec = pl.BlockSpec((tm, d), lambda i: (i, 0))
    return pl.pallas_call(
        _ffn_kernel,
        grid=(rows // tm,),
        in_specs=[row_spec, _const_spec((1, d)), _const_spec(w1.shape), _const_spec(w3.shape),
                  _const_spec(w2.shape), _const_spec((1, d))],
        out_specs=row_spec,
        out_shape=jax.ShapeDtypeStruct((rows, d), F32),
        compiler_params=_params(("parallel",)),
        name="ffn",
    )(x, g, w1, w3, w2, gf)


def _trunk(x, mk, mv, hg0, rw0, sh0, lb, w):
    bsz, t_len, d = x.shape
    x2d = x.reshape(bsz * t_len, d)
    p_hg, p_rw = _norm_matmul(x2d, w['norm_mix'], [w['w_in_hg'], w['w_in_rw']], "in_proj")
    hg_o, hg_s = _hgrn(p_hg.reshape(bsz, t_len, HG_COLS), hg0, lb, w['hgrn_norm'])
    rw_o, rw_s, rw_sh = _rwkv(p_rw.reshape(bsz, t_len, RW_COLS), rw0, sh0, w['rw_vecs'],
                              w['rw_w_b'], w['rw_a_b'], w['rw_g_b'])
    x3d = _cross(x, hg_o, rw_o, mk, mv, w['w_out_hg'], w['w_out_rw'], w['norm_cross'], w['w_cq'], w['w_co'])
    y = _ffn(x3d.reshape(bsz * t_len, d), w['norm_ffn'], w['w_ff1'], w['w_ff3'], w['w_ff2'], w['norm_final'])
    return y.reshape(bsz, t_len, d), hg_s, rw_s, rw_sh


def kernel(x_prompt, mem_prompt, x_sample, cache_mem_k, cache_mem_v, state_hgrn, state_rwkv, state_rwkv_shift, hgrn_lb_logits, norm_mix, w_in, hgrn_norm, rw_mu, rw_w0, rw_w_b, rw_a0, rw_a_b, rw_g_b, rw_k_k, rw_k_a, rw_r_k, rw_gn_w, rw_gn_b, w_out, norm_cross, norm_mem, w_cq, w_ck, w_cv, w_co, norm_ffn, w_ff1, w_ff3, w_ff2, norm_final):
    assert w_in.shape[0] == 1, "single-layer trunk"
    bp = x_prompt.shape[0]
    row = lambda a: a.reshape(1, -1).astype(F32)
    bf = lambda a: a.astype(BF16)
    lb = jnp.cumsum(jax.nn.softmax(hgrn_lb_logits.astype(F32), axis=0), axis=0)[0:1]
    w = {
        'norm_mix': row(norm_mix[0]), 'w_in_hg': bf(w_in[0, :, :HG_COLS]), 'w_in_rw': bf(w_in[0, :, HG_COLS:]),
        'hgrn_norm': row(hgrn_norm[0]),
        'rw_vecs': tuple(row(a[0]) for a in (rw_mu, rw_w0, rw_a0, rw_k_k, rw_k_a, rw_r_k, rw_gn_w, rw_gn_b)),
        'rw_w_b': bf(rw_w_b[0]), 'rw_a_b': bf(rw_a_b[0]), 'rw_g_b': bf(rw_g_b[0]),
        'w_out_hg': bf(w_out[0, :HG_WIDTH]), 'w_out_rw': bf(w_out[0, HG_WIDTH:]),
        'norm_cross': row(norm_cross[0]), 'w_cq': bf(w_cq[0]), 'w_co': bf(w_co[0]),
        'norm_ffn': row(norm_ffn[0]), 'w_ff1': bf(w_ff1[0]), 'w_ff3': bf(w_ff3[0]), 'w_ff2': bf(w_ff2[0]),
        'norm_final': row(norm_final),
    }
    mk, mv = _norm_matmul(mem_prompt.reshape(bp * N_MEM, D_MODEL), row(norm_mem[0]),
                          [bf(w_ck[0]), bf(w_cv[0])], "mem_kv")
    mk = mk.reshape(bp, N_MEM, D_MODEL)
    mv = mv.reshape(bp, N_MEM, D_MODEL)
    y_p, p_hg, p_rw, p_sh = _trunk(
        x_prompt, mk, mv, jnp.zeros((bp, HG_HEADS, HG_DIM, HG_DIM), F32),
        jnp.zeros((bp, RW_HEADS, RW_HEAD, RW_HEAD), F32), jnp.zeros((bp, 1, RW_COLS), F32), lb, w)
    bs = x_sample.shape[0]
    y_s, s_hg, s_rw, s_sh = _trunk(
        x_sample, cache_mem_k[0].reshape(bs, N_MEM, D_MODEL), cache_mem_v[0].reshape(bs, N_MEM, D_MODEL),
        state_hgrn[0], state_rwkv[0], state_rwkv_shift[0], lb, w)
    return (y_p, y_s, p_hg[None], p_rw[None], p_sh[None],
            mk.reshape(1, bp, N_MEM, X_HEADS, X_DIM), mv.reshape(1, bp, N_MEM, X_HEADS, X_DIM),
            s_hg[None], s_rw[None], s_sh[None])
```

```python
import functools

import jax
import jax.numpy as jnp
from jax import lax
from jax.experimental import pallas as pl
from jax.experimental.pallas import tpu as pltpu

F32 = jnp.float32
BF16 = jnp.bfloat16

D_MODEL = 1024
HG_WIDTH = 512
HG_HEADS = 4
HG_DIM = 128
HG_BLOCK = 16
HG_COLS = 4 * HG_WIDTH
RW_WIDTH = 512
RW_HEAD = 64
RW_HEADS = 8
RW_DECAY_LORA = 64
RW_A_LORA = 64
RW_GATE_LORA = 128
RW_COLS = 3 * RW_WIDTH + RW_DECAY_LORA + RW_A_LORA + RW_GATE_LORA
N_MEM = 256
X_HEADS = 4
X_DIM = D_MODEL // X_HEADS
D_FF = 2816
RMS_EPS = 1e-6
GN_EPS = 64e-5

TIME_CHUNK = 64
TIME_BLOCK = 512
HG_CHUNK_GROUP = 4
RW_CHUNK_GROUP = 4
TOKEN_TILE = 512
FF_CHUNK = 1408
COL_CHUNK = 1024
VMEM_LIMIT_BYTES = 56 * 1024 * 1024

NN = (((1,), (0,)), ((), ()))
NT = (((1,), (1,)), ((), ()))
TN = (((0,), (0,)), ((), ()))


def _bdot(a, b, dims=NN):
    return lax.dot_general(a.astype(BF16), b.astype(BF16), dims, preferred_element_type=F32)


def _split3(x):
    hi = x.astype(BF16)
    r1 = x - hi.astype(F32)
    mid = r1.astype(BF16)
    lo = (r1 - mid.astype(F32)).astype(BF16)
    return hi, mid, lo


def _cumdot(mask3, x):
    return jnp.dot(mask3, jnp.concatenate(_split3(x), axis=0), preferred_element_type=F32)


def _head_sums(x):
    pair = 2 * RW_HEAD
    low = lax.broadcasted_iota(jnp.int32, (x.shape[0], pair), 1) < RW_HEAD
    out = []
    for c0 in range(0, x.shape[1], pair):
        xp = x[:, c0:c0 + pair]
        s_low = jnp.sum(jnp.where(low, xp, 0.0), axis=1, keepdims=True)
        s_high = jnp.sum(jnp.where(low, 0.0, xp), axis=1, keepdims=True)
        out.append(jnp.where(low, s_low, s_high))
    return jnp.concatenate(out, axis=1)


def _rmsnorm(x, g):
    return x * lax.rsqrt(jnp.mean(x * x, axis=-1, keepdims=True) + RMS_EPS) * g


def _softplus(x):
    return jnp.maximum(x, 0.0) + jnp.log(1.0 + jnp.exp(-jnp.abs(x)))


def _params(semantics):
    return pltpu.CompilerParams(dimension_semantics=semantics, vmem_limit_bytes=VMEM_LIMIT_BYTES)


def _const_spec(shape):
    zeros = (0,) * len(shape)
    return pl.BlockSpec(shape, lambda *_: zeros)


def _norm_matmul_kernel(x_ref, g_ref, *refs):
    n_out = len(refs) // 2
    w_refs, o_refs = refs[:n_out], refs[n_out:]
    n = _rmsnorm(x_ref[...], g_ref[...]).astype(BF16)
    for w_ref, o_ref in zip(w_refs, o_refs):
        cols = w_ref.shape[1]
        for c0 in range(0, cols, COL_CHUNK):
            c1 = min(c0 + COL_CHUNK, cols)
            o_ref[:, c0:c1] = jnp.dot(n, w_ref[:, c0:c1], preferred_element_type=F32).astype(o_ref.dtype)


def _norm_matmul(x, g, ws, name):
    rows, k = x.shape
    tm = min(TOKEN_TILE, rows)
    return pl.pallas_call(
        _norm_matmul_kernel,
        grid=(rows // tm,),
        in_specs=[pl.BlockSpec((tm, k), lambda i: (i, 0)), _const_spec((1, k))]
        + [_const_spec(w.shape) for w in ws],
        out_specs=[pl.BlockSpec((tm, w.shape[1]), lambda i: (i, 0)) for w in ws],
        out_shape=[jax.ShapeDtypeStruct((rows, w.shape[1]), F32) for w in ws],
        compiler_params=_params(("parallel",)),
        name=name,
    )(x, g, *ws)


def _hgrn_kernel(p_ref, s0_ref, lb_ref, gn_ref, o_ref, sn_ref, st_scr, *, chunk):
    n_blk = chunk // HG_BLOCK
    W = HG_WIDTH
    heads = [slice(h * HG_DIM, (h + 1) * HG_DIM) for h in range(HG_HEADS)]
    hr = range(HG_HEADS)

    @pl.when(pl.program_id(1) == 0)
    def _():
        for h in hr:
            st_scr[h] = s0_ref[0, h].T

    ti = lax.broadcasted_iota(jnp.int32, (chunk, chunk), 0)
    tj = lax.broadcasted_iota(jnp.int32, (chunk, chunk), 1)
    diag_mask = (ti >= tj) & (ti // HG_BLOCK == tj // HG_BLOCK)
    ri = lax.broadcasted_iota(jnp.int32, (3 * chunk, 3 * chunk), 0)
    ci = lax.broadcasted_iota(jnp.int32, (3 * chunk, 3 * chunk), 1) % chunk
    kind, rt = ri // chunk, ri % chunk
    same = rt // HG_BLOCK == ci // HG_BLOCK
    as_f32 = lambda m: m.astype(F32)
    cum_mask = jnp.where(kind == 0, as_f32(rt >= ci),
                         jnp.where(kind == 1, as_f32((rt >= ci) & same),
                                   as_f32(ci // HG_BLOCK <= rt // HG_BLOCK))).astype(BF16)
    n_chunks = p_ref.shape[1] // chunk
    group = HG_CHUNK_GROUP if n_chunks % HG_CHUNK_GROUP == 0 else 1
    R = group * chunk
    cr = range(group)
    units = [(c, h) for c in cr for h in hr]
    ur = range(len(units))
    row_blk = (lax.broadcasted_iota(jnp.int32, (R, 1), 0) % chunk) // HG_BLOCK
    lb = lb_ref[...]
    gn = gn_ref[...]

    def rows(x, c):
        return x[c * chunk:(c + 1) * chunk]

    def part(x, unit):
        c, h = unit
        return x[c * chunk:(c + 1) * chunk, heads[h]]

    def one_group(i, carry):
        off = pl.multiple_of(i * R, R)
        p = p_ref[0, pl.ds(off, R), :]
        q, fpre, v, gate = p[:, :W], p[:, W:2 * W], p[:, 2 * W:3 * W], p[:, 3 * W:]
        sig = jax.nn.sigmoid(fpre)
        logf = jnp.log(lb + (1.0 - lb) * sig)
        k = (1.0 - lb) * (1.0 - sig)
        cums = [_cumdot(cum_mask, rows(logf, c)) for c in cr]
        stack = lambda pieces: jnp.concatenate(pieces, axis=0) if len(pieces) > 1 else pieces[0]
        lc = stack([cums[c][:chunk] for c in cr])
        lblk = stack([cums[c][chunk:2 * chunk] for c in cr])
        lend = stack([cums[c][2 * chunk:] for c in cr])
        llast = [cums[c][chunk - 1:chunk, :] for c in cr]
        to_end = stack([llast[c] - cums[c][:chunk] for c in cr])

        q_in = q * jnp.exp(lc)
        qg = q * jnp.exp(lblk)
        kg = k * jnp.exp(-lblk)
        kd = k * jnp.exp(lend - lc)
        k_end = k * jnp.exp(to_end)
        g_chunk = [jnp.exp(llast[c]) for c in cr]
        q_cross, k_cross = [], []
        for j in range(n_blk - 1):
            end_j = stack([jnp.broadcast_to(cums[c][2 * chunk + j * HG_BLOCK:2 * chunk + j * HG_BLOCK + 1, :],
                                            (chunk, W)) for c in cr])
            q_cross.append(jnp.where(row_blk > j, q * jnp.exp(jnp.minimum(lc - end_j, 0.0)), 0.0).astype(BF16))
            k_cross.append(jnp.where(row_blk == j, kd, 0.0).astype(BF16))

        att = [jnp.where(diag_mask, _bdot(part(qg, u), part(kg, u), NT), 0.0) for u in units]
        if n_blk > 1:
            att = [att[n] + _bdot(jnp.concatenate([part(qc, units[n]) for qc in q_cross], axis=1),
                                  jnp.concatenate([part(kc, units[n]) for kc in k_cross], axis=1), NT) for n in ur]
        o_intra = [_bdot(att[n], part(v, units[n])) for n in ur]
        kv = [_bdot(part(v, u), part(k_end, u), TN) for u in units]
        st = [st_scr[h] for h in hr]
        for c in cr:
            base = c * HG_HEADS
            o = [_bdot(part(q_in, (c, h)), st[h], NT) + o_intra[base + h] for h in hr]
            st = [st[h] * g_chunk[c][:, heads[h]] + kv[base + h] for h in hr]
            for h in hr:
                on = o[h] * lax.rsqrt(jnp.mean(o[h] * o[h], axis=-1, keepdims=True) + RMS_EPS)
                o_ref[0, pl.ds(off + c * chunk, chunk), heads[h]] = (
                    on * gn[:, heads[h]] * jax.nn.silu(part(gate, (c, h)))).astype(o_ref.dtype)
        for h in hr:
            st_scr[h] = st[h]
        return carry

    lax.fori_loop(0, n_chunks // group, one_group, 0)

    @pl.when(pl.program_id(1) == pl.num_programs(1) - 1)
    def _():
        for h in hr:
            sn_ref[0, h] = st_scr[h].T


def _hgrn(p, s0, lb, gn):
    bsz, t_len, _ = p.shape
    chunk = min(TIME_CHUNK, t_len)
    t_blk = min(TIME_BLOCK, t_len)
    return pl.pallas_call(
        functools.partial(_hgrn_kernel, chunk=chunk),
        grid=(bsz, t_len // t_blk),
        in_specs=[pl.BlockSpec((1, t_blk, HG_COLS), lambda b, c: (b, c, 0)),
                  pl.BlockSpec((1, HG_HEADS, HG_DIM, HG_DIM), lambda b, c: (b, 0, 0, 0)),
                  _const_spec((1, HG_WIDTH)), _const_spec((1, HG_WIDTH))],
        out_specs=[pl.BlockSpec((1, t_blk, HG_WIDTH), lambda b, c: (b, c, 0)),
                   pl.BlockSpec((1, HG_HEADS, HG_DIM, HG_DIM), lambda b, c: (b, 0, 0, 0))],
        out_shape=[jax.ShapeDtypeStruct((bsz, t_len, HG_WIDTH), BF16),
                   jax.ShapeDtypeStruct((bsz, HG_HEADS, HG_DIM, HG_DIM), F32)],
        scratch_shapes=[pltpu.VMEM((HG_HEADS, HG_DIM, HG_DIM), F32)],
        compiler_params=_params(("parallel", "arbitrary")),
        name="hgrn",
    )(p, s0, lb, gn)


def _rwkv_kernel(p_ref, s0_ref, sh0_ref, mu_ref, w0_ref, wb_ref, a0_ref, ab_ref, gb_ref, kk_ref, ka_ref,
                 rk_ref, gnw_ref, gnb_ref, o_ref, sn_ref, shn_ref, s_scr, sh_scr, *, chunk):
    W, N, C = RW_WIDTH, RW_HEAD, chunk
    heads = [slice(h * N, (h + 1) * N) for h in range(RW_HEADS)]
    hr = range(RW_HEADS)

    @pl.when(pl.program_id(1) == 0)
    def _():
        s_scr[...] = s0_ref[0]
        sh_scr[...] = sh0_ref[0]

    n_chunks = p_ref.shape[1] // C
    group = RW_CHUNK_GROUP if n_chunks % RW_CHUNK_GROUP == 0 else 1
    R = group * C
    cr = range(group)
    units = [(ci, h) for ci in cr for h in hr]
    ur = range(len(units))
    row = lax.broadcasted_iota(jnp.int32, (R, 1), 0)
    ti = lax.broadcasted_iota(jnp.int32, (C, C), 0)
    tj = lax.broadcasted_iota(jnp.int32, (C, C), 1)
    incl = ti >= tj
    strict = ti > tj
    mi = lax.broadcasted_iota(jnp.int32, (R, 3 * R), 0)
    mj = lax.broadcasted_iota(jnp.int32, (R, 3 * R), 1) % R
    cum_mask = ((mi >= mj) & (mi // C == mj // C)).astype(BF16)

    def part(x, unit):
        ci, h = unit
        return x[ci * C:(ci + 1) * C, heads[h]]

    def one_group(i, carry):
        off = pl.multiple_of(i * R, R)
        p = p_ref[0, pl.ds(off, R), :]
        prev = jnp.where(row == 0, sh_scr[...], pltpu.roll(p, 1, 0))
        sh_scr[...] = p[R - 1:R, :]
        xs = p + (prev - p) * mu_ref[...]
        r, k, v = xs[:, :W], xs[:, W:2 * W], xs[:, 2 * W:3 * W]
        wd = xs[:, 3 * W:3 * W + RW_DECAY_LORA]
        ad = xs[:, 3 * W + RW_DECAY_LORA:3 * W + RW_DECAY_LORA + RW_A_LORA]
        gd = xs[:, 3 * W + RW_DECAY_LORA + RW_A_LORA:]

        w_log = -_softplus(-(w0_ref[...] + _bdot(jnp.tanh(wd), wb_ref[...]))) - 0.5
        ld = -jnp.exp(w_log)
        a_icl = jax.nn.sigmoid(a0_ref[...] + _bdot(ad, ab_ref[...]))
        gate = _bdot(jax.nn.sigmoid(gd), gb_ref[...])
        kk = k * kk_ref[...]
        kk = kk / jnp.maximum(jnp.sqrt(_head_sums(kk * kk)), 1e-12)
        k2 = k * (1.0 + (a_icl - 1.0) * ka_ref[...])
        av = -kk
        bv = kk * a_icl

        lc = _cumdot(cum_mask, ld)
        llast = [lc[ci * C + C - 1:ci * C + C, :] for ci in cr]
        to_end = jnp.concatenate([llast[ci] - lc[ci * C:(ci + 1) * C] for ci in cr], axis=0)
        e_in = jnp.exp(lc)
        e_inv = jnp.exp(-lc)
        e_end = jnp.exp(to_end)
        a_t = av * jnp.exp(lc - ld)
        r_t = r * e_in
        b_t = bv * e_inv
        k_t = k2 * e_inv
        b_h = bv * e_end
        k_h = k2 * e_end
        g_chunk = [jnp.exp(llast[ci]) for ci in cr]

        g = [_bdot(jnp.concatenate([part(a_t, u), part(r_t, u)], axis=0),
                   jnp.concatenate([part(b_t, u), part(k_t, u)], axis=0), NT) for u in units]
        l_pow = [jnp.where(strict, g[n][:C, :C], 0.0).astype(BF16) for n in ur]
        a_rb = [jnp.where(incl, g[n][C:, :C], 0.0).astype(BF16) for n in ur]
        vh = [part(v, u).astype(BF16) for u in units]
        kv = [_bdot(jnp.concatenate([jnp.where(strict, g[n][:C, C:], 0.0),
                                     jnp.where(incl, g[n][C:, C:], 0.0)], axis=0), vh[n]) for n in ur]
        au = [jnp.concatenate([part(a_t, units[n]), kv[n][:C]], axis=1) for n in ur]
        span = 1
        while 2 * span < C:
            res = [_bdot(l_pow[n], jnp.concatenate([au[n].astype(BF16), l_pow[n]], axis=1)) for n in ur]
            au = [au[n] + res[n][:, :2 * N] for n in ur]
            l_pow = [res[n][:, 2 * N:].astype(BF16) for n in ur]
            span *= 2
        au = [au[n] + _bdot(l_pow[n], au[n]) for n in ur]
        mn = [_bdot(au[n], part(b_h, units[n]), TN) for n in ur]
        nt = [mn[n][N:] + _bdot(vh[n], part(k_h, units[n]), TN) for n in ur]
        ry = [_bdot(a_rb[n], au[n]) for n in ur]
        y0 = [ry[n][:, N:] + kv[n][C:] for n in ur]
        r_hat = [part(r_t, units[n]) + ry[n][:, :N] for n in ur]
        s_cur = [s_scr[h] for h in hr]
        ys = []
        for ci in cr:
            base = ci * RW_HEADS
            ys.append([_bdot(r_hat[base + h], s_cur[h], NT) + y0[base + h] for h in hr])
            s_cur = [s_cur[h] * g_chunk[ci][:, heads[h]] + _bdot(s_cur[h], mn[base + h][:N]) + nt[base + h]
                     for h in hr]
        for h in hr:
            s_scr[h] = s_cur[h]

        y = jnp.concatenate([jnp.concatenate(ys[ci], axis=1) for ci in cr], axis=0)
        mean = _head_sums(y) * (1.0 / N)
        yc = y - mean
        var = _head_sums(yc * yc) * (1.0 / N)
        yn = yc * lax.rsqrt(var + GN_EPS) * gnw_ref[...] + gnb_ref[...]
        bonus = _head_sums(r * k2 * rk_ref[...]) * v
        o_ref[0, pl.ds(off, R), :] = ((yn + bonus) * gate).astype(o_ref.dtype)
        return carry

    lax.fori_loop(0, n_chunks // group, one_group, 0)

    @pl.when(pl.program_id(1) == pl.num_programs(1) - 1)
    def _():
        sn_ref[0] = s_scr[...]
        shn_ref[0] = sh_scr[...]


def _rwkv(p, s0, sh0, vecs, wb, ab, gb):
    bsz, t_len, _ = p.shape
    chunk = min(TIME_CHUNK, t_len)
    t_blk = min(TIME_BLOCK, t_len)
    mu, w0, a0, kk, ka, rk, gnw, gnb = vecs
    state_spec = pl.BlockSpec((1, RW_HEADS, RW_HEAD, RW_HEAD), lambda b, c: (b, 0, 0, 0))
    shift_spec = pl.BlockSpec((1, 1, RW_COLS), lambda b, c: (b, 0, 0))
    return pl.pallas_call(
        functools.partial(_rwkv_kernel, chunk=chunk),
        grid=(bsz, t_len // t_blk),
        in_specs=[pl.BlockSpec((1, t_blk, RW_COLS), lambda b, c: (b, c, 0)), state_spec, shift_spec,
                  _const_spec(mu.shape), _const_spec(w0.shape), _const_spec(wb.shape), _const_spec(a0.shape),
                  _const_spec(ab.shape), _const_spec(gb.shape), _const_spec(kk.shape), _const_spec(ka.shape),
                  _const_spec(rk.shape), _const_spec(gnw.shape), _const_spec(gnb.shape)],
        out_specs=[pl.BlockSpec((1, t_blk, RW_WIDTH), lambda b, c: (b, c, 0)), state_spec, shift_spec],
        out_shape=[jax.ShapeDtypeStruct((bsz, t_len, RW_WIDTH), BF16),
                   jax.ShapeDtypeStruct((bsz, RW_HEADS, RW_HEAD, RW_HEAD), F32),
                   jax.ShapeDtypeStruct((bsz, 1, RW_COLS), F32)],
        scratch_shapes=[pltpu.VMEM((RW_HEADS, RW_HEAD, RW_HEAD), F32), pltpu.VMEM((1, RW_COLS), F32)],
        compiler_params=_params(("parallel", "arbitrary")),
        name="rwkv",
    )(p, s0, sh0, mu, w0, wb, a0, ab, gb, kk, ka, rk, gnw, gnb)


def _cross_kernel(x_ref, a_ref, b_ref, k_ref, v_ref, wa_ref, wb_ref, g_ref, wq_ref, wo_ref, o_ref):
    x = x_ref[0] + jnp.dot(a_ref[0], wa_ref[...], preferred_element_type=F32)
    x = x + jnp.dot(b_ref[0], wb_ref[...], preferred_element_type=F32)
    q = jnp.dot(_rmsnorm(x, g_ref[...]).astype(BF16), wq_ref[...], preferred_element_type=F32)
    heads = []
    for h in range(X_HEADS):
        sl = slice(h * X_DIM, (h + 1) * X_DIM)
        s = _bdot(q[:, sl], k_ref[0, :, sl], NT) * (X_DIM ** -0.5)
        e = jnp.exp(s - jnp.max(s, axis=-1, keepdims=True))
        pr = e / jnp.sum(e, axis=-1, keepdims=True)
        heads.append(_bdot(pr, v_ref[0, :, sl]).astype(BF16))
    o_ref[0] = x + jnp.dot(jnp.concatenate(heads, axis=1), wo_ref[...], preferred_element_type=F32)


def _cross(x, a, b, mk, mv, wa, wb, g, wq, wo):
    bsz, t_len, d = x.shape
    tq = min(TOKEN_TILE, t_len)
    tok_spec = lambda width: pl.BlockSpec((1, tq, width), lambda i, t: (i, t, 0))
    mem_spec = pl.BlockSpec((1, N_MEM, d), lambda i, t: (i, 0, 0))
    return pl.pallas_call(
        _cross_kernel,
        grid=(bsz, t_len // tq),
        in_specs=[tok_spec(d), tok_spec(a.shape[2]), tok_spec(b.shape[2]), mem_spec, mem_spec,
                  _const_spec(wa.shape), _const_spec(wb.shape), _const_spec((1, d)),
                  _const_spec(wq.shape), _const_spec(wo.shape)],
        out_specs=tok_spec(d),
        out_shape=jax.ShapeDtypeStruct(x.shape, F32),
        compiler_params=_params(("parallel", "parallel")),
        name="cross",
    )(x, a, b, mk, mv, wa, wb, g, wq, wo)


def _ffn_kernel(x_ref, g_ref, w1_ref, w3_ref, w2_ref, gf_ref, o_ref):
    x = x_ref[...]
    n = _rmsnorm(x, g_ref[...]).astype(BF16)
    acc = x
    for c0 in range(0, D_FF, FF_CHUNK):
        c1 = c0 + FF_CHUNK
        up = jnp.dot(n, w1_ref[:, c0:c1], preferred_element_type=F32)
        lin = jnp.dot(n, w3_ref[:, c0:c1], preferred_element_type=F32)
        hid = (jax.nn.silu(up) * lin).astype(BF16)
        acc = acc + jnp.dot(hid, w2_ref[c0:c1, :], preferred_element_type=F32)
    o_ref[...] = _rmsnorm(acc, gf_ref[...])


def _ffn(x, g, w1, w3, w2, gf):
    rows, d = x.shape
    tm = min(TOKEN_TILE, rows)
    row_spec = pl.BlockSpec((tm, d), lambda i: (i, 0))
    return pl.pallas_call(
        _ffn_kernel,
        grid=(rows // tm,),
        in_specs=[row_spec, _const_spec((1, d)), _const_spec(w1.shape), _const_spec(w3.shape),
                  _const_spec(w2.shape), _const_spec((1, d))],
        out_specs=row_spec,
        out_shape=jax.ShapeDtypeStruct((rows, d), F32),
        compiler_params=_params(("parallel",)),
        name="ffn",
    )(x, g, w1, w3, w2, gf)


def _trunk(x, mk, mv, hg0, rw0, sh0, lb, w):
    bsz, t_len, d = x.shape
    x2d = x.reshape(bsz * t_len, d)
    p_hg, p_rw = _norm_matmul(x2d, w['norm_mix'], [w['w_in_hg'], w['w_in_rw']], "in_proj")
    hg_o, hg_s = _hgrn(p_hg.reshape(bsz, t_len, HG_COLS), hg0, lb, w['hgrn_norm'])
    rw_o, rw_s, rw_sh = _rwkv(p_rw.reshape(bsz, t_len, RW_COLS), rw0, sh0, w['rw_vecs'],
                              w['rw_w_b'], w['rw_a_b'], w['rw_g_b'])
    x3d = _cross(x, hg_o, rw_o, mk, mv, w['w_out_hg'], w['w_out_rw'], w['norm_cross'], w['w_cq'], w['w_co'])
    y = _ffn(x3d.reshape(bsz * t_len, d), w['norm_ffn'], w['w_ff1'], w['w_ff3'], w['w_ff2'], w['norm_final'])
    return y.reshape(bsz, t_len, d), hg_s, rw_s, rw_sh


def kernel(x_prompt, mem_prompt, x_sample, cache_mem_k, cache_mem_v, state_hgrn, state_rwkv, state_rwkv_shift, hgrn_lb_logits, norm_mix, w_in, hgrn_norm, rw_mu, rw_w0, rw_w_b, rw_a0, rw_a_b, rw_g_b, rw_k_k, rw_k_a, rw_r_k, rw_gn_w, rw_gn_b, w_out, norm_cross, norm_mem, w_cq, w_ck, w_cv, w_co, norm_ffn, w_ff1, w_ff3, w_ff2, norm_final):
    assert w_in.shape[0] == 1, "single-layer trunk"
    bp = x_prompt.shape[0]
    row = lambda a: a.reshape(1, -1).astype(F32)
    bf = lambda a: a.astype(BF16)
    lb = jnp.cumsum(jax.nn.softmax(hgrn_lb_logits.astype(F32), axis=0), axis=0)[0:1]
    w = {
        'norm_mix': row(norm_mix[0]), 'w_in_hg': bf(w_in[0, :, :HG_COLS]), 'w_in_rw': bf(w_in[0, :, HG_COLS:]),
        'hgrn_norm': row(hgrn_norm[0]),
        'rw_vecs': tuple(row(a[0]) for a in (rw_mu, rw_w0, rw_a0, rw_k_k, rw_k_a, rw_r_k, rw_gn_w, rw_gn_b)),
        'rw_w_b': bf(rw_w_b[0]), 'rw_a_b': bf(rw_a_b[0]), 'rw_g_b': bf(rw_g_b[0]),
        'w_out_hg': bf(w_out[0, :HG_WIDTH]), 'w_out_rw': bf(w_out[0, HG_WIDTH:]),
        'norm_cross': row(norm_cross[0]), 'w_cq': bf(w_cq[0]), 'w_co': bf(w_co[0]),
        'norm_ffn': row(norm_ffn[0]), 'w_ff1': bf(w_ff1[0]), 'w_ff3': bf(w_ff3[0]), 'w_ff2': bf(w_ff2[0]),
        'norm_final': row(norm_final),
    }
    mk, mv = _norm_matmul(mem_prompt.reshape(bp * N_MEM, D_MODEL), row(norm_mem[0]),
                          [bf(w_ck[0]), bf(w_cv[0])], "mem_kv")
    mk = mk.reshape(bp, N_MEM, D_MODEL)
    mv = mv.reshape(bp, N_MEM, D_MODEL)
    y_p, p_hg, p_rw, p_sh = _trunk(
        x_prompt, mk, mv, jnp.zeros((bp, HG_HEADS, HG_DIM, HG_DIM), F32),
        jnp.zeros((bp, RW_HEADS, RW_HEAD, RW_HEAD), F32), jnp.zeros((bp, 1, RW_COLS), F32), lb, w)
    bs = x_sample.shape[0]
    y_s, s_hg, s_rw, s_sh = _trunk(
        x_sample, cache_mem_k[0].reshape(bs, N_MEM, D_MODEL), cache_mem_v[0].reshape(bs, N_MEM, D_MODEL),
        state_hgrn[0], state_rwkv[0], state_rwkv_shift[0], lb, w)
    return (y_p, y_s, p_hg[None], p_rw[None], p_sh[None],
            mk.reshape(1, bp, N_MEM, X_HEADS, X_DIM), mv.reshape(1, bp, N_MEM, X_HEADS, X_DIM),
            s_hg[None], s_rw[None], s_sh[None])
```

```python
import functools

import jax
import jax.numpy as jnp
from jax import lax
from jax.experimental import pallas as pl
from jax.experimental.pallas import tpu as pltpu

F32 = jnp.float32
BF16 = jnp.bfloat16

D_MODEL = 1024
HG_WIDTH = 512
HG_HEADS = 4
HG_DIM = 128
HG_BLOCK = 16
HG_COLS = 4 * HG_WIDTH
RW_WIDTH = 512
RW_HEAD = 64
RW_HEADS = 8
RW_DECAY_LORA = 64
RW_A_LORA = 64
RW_GATE_LORA = 128
RW_COLS = 3 * RW_WIDTH + RW_DECAY_LORA + RW_A_LORA + RW_GATE_LORA
N_MEM = 256
X_HEADS = 4
X_DIM = D_MODEL // X_HEADS
D_FF = 2816
RMS_EPS = 1e-6
GN_EPS = 64e-5

TIME_CHUNK = 64
TIME_BLOCK = 512
HG_CHUNK_GROUP = 4
RW_CHUNK_GROUP = 4
TOKEN_TILE = 512
FFN_TILE = 1024
FF_CHUNK = 256
COL_CHUNK = 1024
VMEM_LIMIT_BYTES = 56 * 1024 * 1024

NN = (((1,), (0,)), ((), ()))
NT = (((1,), (1,)), ((), ()))
TN = (((0,), (0,)), ((), ()))


def _bdot(a, b, dims=NN):
    return lax.dot_general(a.astype(BF16), b.astype(BF16), dims, preferred_element_type=F32)


def _split3(x):
    hi = x.astype(BF16)
    r1 = x - hi.astype(F32)
    mid = r1.astype(BF16)
    lo = (r1 - mid.astype(F32)).astype(BF16)
    return hi, mid, lo


def _cumdot(mask3, x):
    return jnp.dot(mask3, jnp.concatenate(_split3(x), axis=0), preferred_element_type=F32)


def _head_sums(x):
    pair = 2 * RW_HEAD
    low = lax.broadcasted_iota(jnp.int32, (x.shape[0], pair), 1) < RW_HEAD
    out = []
    for c0 in range(0, x.shape[1], pair):
        xp = x[:, c0:c0 + pair]
        s_low = jnp.sum(jnp.where(low, xp, 0.0), axis=1, keepdims=True)
        s_high = jnp.sum(jnp.where(low, 0.0, xp), axis=1, keepdims=True)
        out.append(jnp.where(low, s_low, s_high))
    return jnp.concatenate(out, axis=1)


def _rmsnorm(x, g):
    return x * lax.rsqrt(jnp.mean(x * x, axis=-1, keepdims=True) + RMS_EPS) * g


def _softplus(x):
    return jnp.maximum(x, 0.0) + jnp.log(1.0 + jnp.exp(-jnp.abs(x)))


def _params(semantics):
    return pltpu.CompilerParams(dimension_semantics=semantics, vmem_limit_bytes=VMEM_LIMIT_BYTES)


def _const_spec(shape):
    zeros = (0,) * len(shape)
    return pl.BlockSpec(shape, lambda *_: zeros)


def _norm_matmul_kernel(x_ref, g_ref, *refs):
    n_out = len(refs) // 2
    w_refs, o_refs = refs[:n_out], refs[n_out:]
    n = _rmsnorm(x_ref[...], g_ref[...]).astype(BF16)
    for w_ref, o_ref in zip(w_refs, o_refs):
        cols = w_ref.shape[1]
        for c0 in range(0, cols, COL_CHUNK):
            c1 = min(c0 + COL_CHUNK, cols)
            o_ref[:, c0:c1] = jnp.dot(n, w_ref[:, c0:c1], preferred_element_type=F32).astype(o_ref.dtype)


def _norm_matmul(x, g, ws, name):
    rows, k = x.shape
    tm = min(TOKEN_TILE, rows)
    return pl.pallas_call(
        _norm_matmul_kernel,
        grid=(rows // tm,),
        in_specs=[pl.BlockSpec((tm, k), lambda i: (i, 0)), _const_spec((1, k))]
        + [_const_spec(w.shape) for w in ws],
        out_specs=[pl.BlockSpec((tm, w.shape[1]), lambda i: (i, 0)) for w in ws],
        out_shape=[jax.ShapeDtypeStruct((rows, w.shape[1]), F32) for w in ws],
        compiler_params=_params(("parallel",)),
        name=name,
    )(x, g, *ws)


def _hgrn_kernel(p_ref, s0_ref, lb_ref, gn_ref, o_ref, sn_ref, st_scr, *, chunk):
    n_blk = chunk // HG_BLOCK
    W = HG_WIDTH
    heads = [slice(h * HG_DIM, (h + 1) * HG_DIM) for h in range(HG_HEADS)]
    hr = range(HG_HEADS)

    @pl.when(pl.program_id(1) == 0)
    def _():
        for h in hr:
            st_scr[h] = s0_ref[0, h].T

    ti = lax.broadcasted_iota(jnp.int32, (chunk, chunk), 0)
    tj = lax.broadcasted_iota(jnp.int32, (chunk, chunk), 1)
    diag_mask = (ti >= tj) & (ti // HG_BLOCK == tj // HG_BLOCK)
    ri = lax.broadcasted_iota(jnp.int32, (3 * chunk, 3 * chunk), 0)
    ci = lax.broadcasted_iota(jnp.int32, (3 * chunk, 3 * chunk), 1) % chunk
    kind, rt = ri // chunk, ri % chunk
    same = rt // HG_BLOCK == ci // HG_BLOCK
    as_f32 = lambda m: m.astype(F32)
    cum_mask = jnp.where(kind == 0, as_f32(rt >= ci),
                         jnp.where(kind == 1, as_f32((rt >= ci) & same),
                                   as_f32(ci // HG_BLOCK <= rt // HG_BLOCK))).astype(BF16)
    n_chunks = p_ref.shape[1] // chunk
    group = HG_CHUNK_GROUP if n_chunks % HG_CHUNK_GROUP == 0 else 1
    R = group * chunk
    cr = range(group)
    units = [(c, h) for c in cr for h in hr]
    ur = range(len(units))
    row_blk = (lax.broadcasted_iota(jnp.int32, (R, 1), 0) % chunk) // HG_BLOCK
    lb = lb_ref[...]
    gn = gn_ref[...]

    def rows(x, c):
        return x[c * chunk:(c + 1) * chunk]

    def part(x, unit):
        c, h = unit
        return x[c * chunk:(c + 1) * chunk, heads[h]]

    def one_group(i, carry):
        off = pl.multiple_of(i * R, R)
        p = p_ref[0, pl.ds(off, R), :]
        q, fpre, v, gate = p[:, :W], p[:, W:2 * W], p[:, 2 * W:3 * W], p[:, 3 * W:]
        sig = jax.nn.sigmoid(fpre)
        logf = jnp.log(lb + (1.0 - lb) * sig)
        k = (1.0 - lb) * (1.0 - sig)
        cums = [_cumdot(cum_mask, rows(logf, c)) for c in cr]
        stack = lambda pieces: jnp.concatenate(pieces, axis=0) if len(pieces) > 1 else pieces[0]
        lc = stack([cums[c][:chunk] for c in cr])
        lblk = stack([cums[c][chunk:2 * chunk] for c in cr])
        lend = stack([cums[c][2 * chunk:] for c in cr])
        llast = [cums[c][chunk - 1:chunk, :] for c in cr]
        to_end = stack([llast[c] - cums[c][:chunk] for c in cr])

        q_in = q * jnp.exp(lc)
        qg = q * jnp.exp(lblk)
        kg = k * jnp.exp(-lblk)
        kd = k * jnp.exp(lend - lc)
        k_end = k * jnp.exp(to_end)
        g_chunk = [jnp.exp(llast[c]) for c in cr]
        q_cross, k_cross = [], []
        for j in range(n_blk - 1):
            end_j = stack([jnp.broadcast_to(cums[c][2 * chunk + j * HG_BLOCK:2 * chunk + j * HG_BLOCK + 1, :],
                                            (chunk, W)) for c in cr])
            q_cross.append(jnp.where(row_blk > j, q * jnp.exp(jnp.minimum(lc - end_j, 0.0)), 0.0).astype(BF16))
            k_cross.append(jnp.where(row_blk == j, kd, 0.0).astype(BF16))

        att = [jnp.where(diag_mask, _bdot(part(qg, u), part(kg, u), NT), 0.0) for u in units]
        if n_blk > 1:
            att = [att[n] + _bdot(jnp.concatenate([part(qc, units[n]) for qc in q_cross], axis=1),
                                  jnp.concatenate([part(kc, units[n]) for kc in k_cross], axis=1), NT) for n in ur]
        o_intra = [_bdot(att[n], part(v, units[n])) for n in ur]
        kv = [_bdot(part(v, u), part(k_end, u), TN) for u in units]
        st = [st_scr[h] for h in hr]
        for c in cr:
            base = c * HG_HEADS
            o = [_bdot(part(q_in, (c, h)), st[h], NT) + o_intra[base + h] for h in hr]
            st = [st[h] * g_chunk[c][:, heads[h]] + kv[base + h] for h in hr]
            for h in hr:
                on = o[h] * lax.rsqrt(jnp.mean(o[h] * o[h], axis=-1, keepdims=True) + RMS_EPS)
                o_ref[0, pl.ds(off + c * chunk, chunk), heads[h]] = (
                    on * gn[:, heads[h]] * jax.nn.silu(part(gate, (c, h)))).astype(o_ref.dtype)
        for h in hr:
            st_scr[h] = st[h]
        return carry

    lax.fori_loop(0, n_chunks // group, one_group, 0)

    @pl.when(pl.program_id(1) == pl.num_programs(1) - 1)
    def _():
        for h in hr:
            sn_ref[0, h] = st_scr[h].T


def _hgrn(p, s0, lb, gn):
    bsz, t_len, _ = p.shape
    chunk = min(TIME_CHUNK, t_len)
    t_blk = min(TIME_BLOCK, t_len)
    return pl.pallas_call(
        functools.partial(_hgrn_kernel, chunk=chunk),
        grid=(bsz, t_len // t_blk),
        in_specs=[pl.BlockSpec((1, t_blk, HG_COLS), lambda b, c: (b, c, 0)),
                  pl.BlockSpec((1, HG_HEADS, HG_DIM, HG_DIM), lambda b, c: (b, 0, 0, 0)),
                  _const_spec((1, HG_WIDTH)), _const_spec((1, HG_WIDTH))],
        out_specs=[pl.BlockSpec((1, t_blk, HG_WIDTH), lambda b, c: (b, c, 0)),
                   pl.BlockSpec((1, HG_HEADS, HG_DIM, HG_DIM), lambda b, c: (b, 0, 0, 0))],
        out_shape=[jax.ShapeDtypeStruct((bsz, t_len, HG_WIDTH), BF16),
                   jax.ShapeDtypeStruct((bsz, HG_HEADS, HG_DIM, HG_DIM), F32)],
        scratch_shapes=[pltpu.VMEM((HG_HEADS, HG_DIM, HG_DIM), F32)],
        compiler_params=_params(("parallel", "arbitrary")),
        name="hgrn",
    )(p, s0, lb, gn)


def _rwkv_kernel(p_ref, s0_ref, sh0_ref, mu_ref, w0_ref, wb_ref, a0_ref, ab_ref, gb_ref, kk_ref, ka_ref,
                 rk_ref, gnw_ref, gnb_ref, o_ref, sn_ref, shn_ref, s_scr, sh_scr, *, chunk):
    W, N, C = RW_WIDTH, RW_HEAD, chunk
    heads = [slice(h * N, (h + 1) * N) for h in range(RW_HEADS)]
    hr = range(RW_HEADS)

    @pl.when(pl.program_id(1) == 0)
    def _():
        s_scr[...] = s0_ref[0]
        sh_scr[...] = sh0_ref[0]

    n_chunks = p_ref.shape[1] // C
    group = RW_CHUNK_GROUP if n_chunks % RW_CHUNK_GROUP == 0 else 1
    R = group * C
    cr = range(group)
    units = [(ci, h) for ci in cr for h in hr]
    ur = range(len(units))
    row = lax.broadcasted_iota(jnp.int32, (R, 1), 0)
    ti = lax.broadcasted_iota(jnp.int32, (C, C), 0)
    tj = lax.broadcasted_iota(jnp.int32, (C, C), 1)
    incl = ti >= tj
    strict = ti > tj
    mi = lax.broadcasted_iota(jnp.int32, (R, 3 * R), 0)
    mj = lax.broadcasted_iota(jnp.int32, (R, 3 * R), 1) % R
    cum_mask = ((mi >= mj) & (mi // C == mj // C)).astype(BF16)

    def part(x, unit):
        ci, h = unit
        return x[ci * C:(ci + 1) * C, heads[h]]

    def one_group(i, carry):
        off = pl.multiple_of(i * R, R)
        p = p_ref[0, pl.ds(off, R), :]
        prev = jnp.where(row == 0, sh_scr[...], pltpu.roll(p, 1, 0))
        sh_scr[...] = p[R - 1:R, :]
        xs = p + (prev - p) * mu_ref[...]
        r, k, v = xs[:, :W], xs[:, W:2 * W], xs[:, 2 * W:3 * W]
        wd = xs[:, 3 * W:3 * W + RW_DECAY_LORA]
        ad = xs[:, 3 * W + RW_DECAY_LORA:3 * W + RW_DECAY_LORA + RW_A_LORA]
        gd = xs[:, 3 * W + RW_DECAY_LORA + RW_A_LORA:]

        w_log = -_softplus(-(w0_ref[...] + _bdot(jnp.tanh(wd), wb_ref[...]))) - 0.5
        ld = -jnp.exp(w_log)
        a_icl = jax.nn.sigmoid(a0_ref[...] + _bdot(ad, ab_ref[...]))
        gate = _bdot(jax.nn.sigmoid(gd), gb_ref[...])
        kk = k * kk_ref[...]
        kk = kk / jnp.maximum(jnp.sqrt(_head_sums(kk * kk)), 1e-12)
        k2 = k * (1.0 + (a_icl - 1.0) * ka_ref[...])
        av = -kk
        bv = kk * a_icl

        lc = _cumdot(cum_mask, ld)
        llast = [lc[ci * C + C - 1:ci * C + C, :] for ci in cr]
        to_end = jnp.concatenate([llast[ci] - lc[ci * C:(ci + 1) * C] for ci in cr], axis=0)
        e_in = jnp.exp(lc)
        e_inv = jnp.exp(-lc)
        e_end = jnp.exp(to_end)
        a_t = av * jnp.exp(lc - ld)
        r_t = r * e_in
        b_t = bv * e_inv
        k_t = k2 * e_inv
        b_h = bv * e_end
        k_h = k2 * e_end
        g_chunk = [jnp.exp(llast[ci]) for ci in cr]

        g = [_bdot(jnp.concatenate([part(a_t, u), part(r_t, u)], axis=0),
                   jnp.concatenate([part(b_t, u), part(k_t, u)], axis=0), NT) for u in units]
        l_pow = [jnp.where(strict, g[n][:C, :C], 0.0).astype(BF16) for n in ur]
        a_rb = [jnp.where(incl, g[n][C:, :C], 0.0).astype(BF16) for n in ur]
        vh = [part(v, u).astype(BF16) for u in units]
        kv = [_bdot(jnp.concatenate([jnp.where(strict, g[n][:C, C:], 0.0),
                                     jnp.where(incl, g[n][C:, C:], 0.0)], axis=0), vh[n]) for n in ur]
        au = [jnp.concatenate([part(a_t, units[n]), kv[n][:C]], axis=1) for n in ur]
        span = 1
        while 2 * span < C:
            res = [_bdot(l_pow[n], jnp.concatenate([au[n].astype(BF16), l_pow[n]], axis=1)) for n in ur]
            au = [au[n] + res[n][:, :2 * N] for n in ur]
            l_pow = [res[n][:, 2 * N:].astype(BF16) for n in ur]
            span *= 2
        au = [au[n] + _bdot(l_pow[n], au[n]) for n in ur]
        mn = [_bdot(au[n], part(b_h, units[n]), TN) for n in ur]
        nt = [mn[n][N:] + _bdot(vh[n], part(k_h, units[n]), TN) for n in ur]
        ry = [_bdot(a_rb[n], au[n]) for n in ur]
        y0 = [ry[n][:, N:] + kv[n][C:] for n in ur]
        r_hat = [part(r_t, units[n]) + ry[n][:, :N] for n in ur]
        s_cur = [s_scr[h] for h in hr]
        ys = []
        for ci in cr:
            base = ci * RW_HEADS
            ys.append([_bdot(r_hat[base + h], s_cur[h], NT) + y0[base + h] for h in hr])
            s_cur = [s_cur[h] * g_chunk[ci][:, heads[h]] + _bdot(s_cur[h], mn[base + h][:N]) + nt[base + h]
                     for h in hr]
        for h in hr:
            s_scr[h] = s_cur[h]

        y = jnp.concatenate([jnp.concatenate(ys[ci], axis=1) for ci in cr], axis=0)
        mean = _head_sums(y) * (1.0 / N)
        yc = y - mean
        var = _head_sums(yc * yc) * (1.0 / N)
        yn = yc * lax.rsqrt(var + GN_EPS) * gnw_ref[...] + gnb_ref[...]
        bonus = _head_sums(r * k2 * rk_ref[...]) * v
        o_ref[0, pl.ds(off, R), :] = ((yn + bonus) * gate).astype(o_ref.dtype)
        return carry

    lax.fori_loop(0, n_chunks // group, one_group, 0)

    @pl.when(pl.program_id(1) == pl.num_programs(1) - 1)
    def _():
        sn_ref[0] = s_scr[...]
        shn_ref[0] = sh_scr[...]


def _rwkv(p, s0, sh0, vecs, wb, ab, gb):
    bsz, t_len, _ = p.shape
    chunk = min(TIME_CHUNK, t_len)
    t_blk = min(TIME_BLOCK, t_len)
    mu, w0, a0, kk, ka, rk, gnw, gnb = vecs
    state_spec = pl.BlockSpec((1, RW_HEADS, RW_HEAD, RW_HEAD), lambda b, c: (b, 0, 0, 0))
    shift_spec = pl.BlockSpec((1, 1, RW_COLS), lambda b, c: (b, 0, 0))
    return pl.pallas_call(
        functools.partial(_rwkv_kernel, chunk=chunk),
        grid=(bsz, t_len // t_blk),
        in_specs=[pl.BlockSpec((1, t_blk, RW_COLS), lambda b, c: (b, c, 0)), state_spec, shift_spec,
                  _const_spec(mu.shape), _const_spec(w0.shape), _const_spec(wb.shape), _const_spec(a0.shape),
                  _const_spec(ab.shape), _const_spec(gb.shape), _const_spec(kk.shape), _const_spec(ka.shape),
                  _const_spec(rk.shape), _const_spec(gnw.shape), _const_spec(gnb.shape)],
        out_specs=[pl.BlockSpec((1, t_blk, RW_WIDTH), lambda b, c: (b, c, 0)), state_spec, shift_spec],
        out_shape=[jax.ShapeDtypeStruct((bsz, t_len, RW_WIDTH), BF16),
                   jax.ShapeDtypeStruct((bsz, RW_HEADS, RW_HEAD, RW_HEAD), F32),
                   jax.ShapeDtypeStruct((bsz, 1, RW_COLS), F32)],
        scratch_shapes=[pltpu.VMEM((RW_HEADS, RW_HEAD, RW_HEAD), F32), pltpu.VMEM((1, RW_COLS), F32)],
        compiler_params=_params(("parallel", "arbitrary")),
        name="rwkv",
    )(p, s0, sh0, mu, w0, wb, a0, ab, gb, kk, ka, rk, gnw, gnb)


def _cross_kernel(x_ref, a_ref, b_ref, k_ref, v_ref, wa_ref, wb_ref, g_ref, wq_ref, wo_ref, o_ref):
    bb, tq, d = x_ref.shape
    rows = bb * tq
    x = x_ref[...].reshape(rows, d)
    x = x + jnp.dot(a_ref[...].reshape(rows, a_ref.shape[2]), wa_ref[...], preferred_element_type=F32)
    x = x + jnp.dot(b_ref[...].reshape(rows, b_ref.shape[2]), wb_ref[...], preferred_element_type=F32)
    q = jnp.dot(_rmsnorm(x, g_ref[...]).astype(BF16), wq_ref[...], preferred_element_type=F32)
    heads = [slice(h * X_DIM, (h + 1) * X_DIM) for h in range(X_HEADS)]
    units = [(i, sl) for i in range(bb) for sl in heads]
    s = [_bdot(q[i * tq:(i + 1) * tq, sl], k_ref[i, :, sl], NT) * (X_DIM ** -0.5) for i, sl in units]
    e = [jnp.exp(sn - jnp.max(sn, axis=-1, keepdims=True)) for sn in s]
    pr = [en / jnp.sum(en, axis=-1, keepdims=True) for en in e]
    o = [_bdot(pr[n], v_ref[i, :, sl]).astype(BF16) for n, (i, sl) in enumerate(units)]
    per_row = [jnp.concatenate(o[i * X_HEADS:(i + 1) * X_HEADS], axis=1) for i in range(bb)]
    attn = jnp.concatenate(per_row, axis=0) if bb > 1 else per_row[0]
    out = x + jnp.dot(attn, wo_ref[...], preferred_element_type=F32)
    o_ref[...] = out.reshape(bb, tq, d)


def _cross(x, a, b, mk, mv, wa, wb, g, wq, wo):
    bsz, t_len, d = x.shape
    tq = min(TOKEN_TILE, t_len)
    bb = min(bsz, TOKEN_TILE // tq)
    tok_spec = lambda width: pl.BlockSpec((bb, tq, width), lambda i, t: (i, t, 0))
    mem_spec = pl.BlockSpec((bb, N_MEM, d), lambda i, t: (i, 0, 0))
    return pl.pallas_call(
        _cross_kernel,
        grid=(bsz // bb, t_len // tq),
        in_specs=[tok_spec(d), tok_spec(a.shape[2]), tok_spec(b.shape[2]), mem_spec, mem_spec,
                  _const_spec(wa.shape), _const_spec(wb.shape), _const_spec((1, d)),
                  _const_spec(wq.shape), _const_spec(wo.shape)],
        out_specs=tok_spec(d),
        out_shape=jax.ShapeDtypeStruct(x.shape, F32),
        compiler_params=_params(("parallel", "parallel")),
        name="cross",
    )(x, a, b, mk, mv, wa, wb, g, wq, wo)


def _ffn_kernel(x_ref, g_ref, w1_ref, w3_ref, w2_ref, gf_ref, o_ref):
    x = x_ref[...]
    n = _rmsnorm(x, g_ref[...]).astype(BF16)
    acc = x
    for c0 in range(0, D_FF, FF_CHUNK):
        c1 = c0 + FF_CHUNK
        up = jnp.dot(n, w1_ref[:, c0:c1], preferred_element_type=F32)
        lin = jnp.dot(n, w3_ref[:, c0:c1], preferred_element_type=F32)
        hid = (jax.nn.silu(up) * lin).astype(BF16)
        acc = acc + jnp.dot(hid, w2_ref[c0:c1, :], preferred_element_type=F32)
    o_ref[...] = _rmsnorm(acc, gf_ref[...])


def _ffn(x, g, w1, w3, w2, gf):
    rows, d = x.shape
    tm = min(FFN_TILE, rows)
    row_spec = pl.BlockSpec((tm, d), lambda i: (i, 0))
    return pl.pallas_call(
        _ffn_kernel,
        grid=(rows // tm,),
        in_specs=[row_spec, _const_spec((1, d)), _const_spec(w1.shape), _const_spec(w3.shape),
                  _const_spec(w2.shape), _const_spec((1, d))],
        out_specs=row_spec,
        out_shape=jax.ShapeDtypeStruct((rows, d), F32),
        compiler_params=_params(("parallel",)),
        name="ffn",
    )(x, g, w1, w3, w2, gf)


def _trunk(x, mk, mv, hg0, rw0, sh0, lb, w):
    bsz, t_len, d = x.shape
    x2d = x.reshape(bsz * t_len, d)
    p_hg, p_rw = _norm_matmul(x2d, w['norm_mix'], [w['w_in_hg'], w['w_in_rw']], "in_proj")
    hg_o, hg_s = _hgrn(p_hg.reshape(bsz, t_len, HG_COLS), hg0, lb, w['hgrn_norm'])
    rw_o, rw_s, rw_sh = _rwkv(p_rw.reshape(bsz, t_len, RW_COLS), rw0, sh0, w['rw_vecs'],
                              w['rw_w_b'], w['rw_a_b'], w['rw_g_b'])
    x3d = _cross(x, hg_o, rw_o, mk, mv, w['w_out_hg'], w['w_out_rw'], w['norm_cross'], w['w_cq'], w['w_co'])
    y = _ffn(x3d.reshape(bsz * t_len, d), w['norm_ffn'], w['w_ff1'], w['w_ff3'], w['w_ff2'], w['norm_final'])
    return y.reshape(bsz, t_len, d), hg_s, rw_s, rw_sh


def kernel(x_prompt, mem_prompt, x_sample, cache_mem_k, cache_mem_v, state_hgrn, state_rwkv, state_rwkv_shift, hgrn_lb_logits, norm_mix, w_in, hgrn_norm, rw_mu, rw_w0, rw_w_b, rw_a0, rw_a_b, rw_g_b, rw_k_k, rw_k_a, rw_r_k, rw_gn_w, rw_gn_b, w_out, norm_cross, norm_mem, w_cq, w_ck, w_cv, w_co, norm_ffn, w_ff1, w_ff3, w_ff2, norm_final):
    assert w_in.shape[0] == 1, "single-layer trunk"
    bp = x_prompt.shape[0]
    row = lambda a: a.reshape(1, -1).astype(F32)
    bf = lambda a: a.astype(BF16)
    lb = jnp.cumsum(jax.nn.softmax(hgrn_lb_logits.astype(F32), axis=0), axis=0)[0:1]
    w = {
        'norm_mix': row(norm_mix[0]), 'w_in_hg': bf(w_in[0, :, :HG_COLS]), 'w_in_rw': bf(w_in[0, :, HG_COLS:]),
        'hgrn_norm': row(hgrn_norm[0]),
        'rw_vecs': tuple(row(a[0]) for a in (rw_mu, rw_w0, rw_a0, rw_k_k, rw_k_a, rw_r_k, rw_gn_w, rw_gn_b)),
        'rw_w_b': bf(rw_w_b[0]), 'rw_a_b': bf(rw_a_b[0]), 'rw_g_b': bf(rw_g_b[0]),
        'w_out_hg': bf(w_out[0, :HG_WIDTH]), 'w_out_rw': bf(w_out[0, HG_WIDTH:]),
        'norm_cross': row(norm_cross[0]), 'w_cq': bf(w_cq[0]), 'w_co': bf(w_co[0]),
        'norm_ffn': row(norm_ffn[0]), 'w_ff1': bf(w_ff1[0]), 'w_ff3': bf(w_ff3[0]), 'w_ff2': bf(w_ff2[0]),
        'norm_final': row(norm_final),
    }
    mk, mv = _norm_matmul(mem_prompt.reshape(bp * N_MEM, D_MODEL), row(norm_mem[0]),
                          [bf(w_ck[0]), bf(w_cv[0])], "mem_kv")
    mk = mk.reshape(bp, N_MEM, D_MODEL)
    mv = mv.reshape(bp, N_MEM, D_MODEL)
    y_p, p_hg, p_rw, p_sh = _trunk(
        x_prompt, mk, mv, jnp.zeros((bp, HG_HEADS, HG_DIM, HG_DIM), F32),
        jnp.zeros((bp, RW_HEADS, RW_HEAD, RW_HEAD), F32), jnp.zeros((bp, 1, RW_COLS), F32), lb, w)
    bs = x_sample.shape[0]
    y_s, s_hg, s_rw, s_sh = _trunk(
        x_sample, cache_mem_k[0].reshape(bs, N_MEM, D_MODEL), cache_mem_v[0].reshape(bs, N_MEM, D_MODEL),
        state_hgrn[0], state_rwkv[0], state_rwkv_shift[0], lb, w)
    return (y_p, y_s, p_hg[None], p_rw[None], p_sh[None],
            mk.reshape(1, bp, N_MEM, X_HEADS, X_DIM), mv.reshape(1, bp, N_MEM, X_HEADS, X_DIM),
            s_hg[None], s_rw[None], s_sh[None])
```

```python
import functools

import jax
import jax.numpy as jnp
from jax import lax
from jax.experimental import pallas as pl
from jax.experimental.pallas import tpu as pltpu

F32 = jnp.float32
BF16 = jnp.bfloat16

D_MODEL = 1024
HG_WIDTH = 512
HG_HEADS = 4
HG_DIM = 128
HG_BLOCK = 16
HG_COLS = 4 * HG_WIDTH
RW_WIDTH = 512
RW_HEAD = 64
RW_HEADS = 8
RW_DECAY_LORA = 64
RW_A_LORA = 64
RW_GATE_LORA = 128
RW_COLS = 3 * RW_WIDTH + RW_DECAY_LORA + RW_A_LORA + RW_GATE_LORA
N_MEM = 256
X_HEADS = 4
X_DIM = D_MODEL // X_HEADS
D_FF = 2816
RMS_EPS = 1e-6
GN_EPS = 64e-5
LOG2_E = 1.4426950408889634

TIME_CHUNK = 64
TIME_BLOCK = 512
HG_CHUNK_GROUP = 8
RW_CHUNK_GROUP = 4
TOKEN_TILE = 512
CROSS_TILE = 1024
FFN_TILE = 1024
FF_CHUNK = 256
COL_CHUNK = 1024
VMEM_LIMIT_BYTES = 56 * 1024 * 1024

NN = (((1,), (0,)), ((), ()))
NT = (((1,), (1,)), ((), ()))
TN = (((0,), (0,)), ((), ()))


def _bdot(a, b, dims=NN):
    return lax.dot_general(a.astype(BF16), b.astype(BF16), dims, preferred_element_type=F32)


def _split3(x):
    hi = x.astype(BF16)
    r1 = x - hi.astype(F32)
    mid = r1.astype(BF16)
    lo = (r1 - mid.astype(F32)).astype(BF16)
    return hi, mid, lo


def _cumdot(mask3, x):
    return jnp.dot(mask3, jnp.concatenate(_split3(x), axis=0), preferred_element_type=F32)


def _head_sums(x):
    pair = 2 * RW_HEAD
    low = lax.broadcasted_iota(jnp.int32, (x.shape[0], pair), 1) < RW_HEAD
    out = []
    for c0 in range(0, x.shape[1], pair):
        xp = x[:, c0:c0 + pair]
        s_low = jnp.sum(jnp.where(low, xp, 0.0), axis=1, keepdims=True)
        s_high = jnp.sum(jnp.where(low, 0.0, xp), axis=1, keepdims=True)
        out.append(jnp.where(low, s_low, s_high))
    return jnp.concatenate(out, axis=1)


def _rmsnorm(x, g):
    return x * lax.rsqrt(jnp.mean(x * x, axis=-1, keepdims=True) + RMS_EPS) * g


def _softplus(x):
    return jnp.maximum(x, 0.0) + jnp.log(1.0 + jnp.exp(-jnp.abs(x)))


def _params(semantics):
    return pltpu.CompilerParams(dimension_semantics=semantics, vmem_limit_bytes=VMEM_LIMIT_BYTES)


def _const_spec(shape):
    zeros = (0,) * len(shape)
    return pl.BlockSpec(shape, lambda *_: zeros)


def _norm_matmul_kernel(x_ref, g_ref, *refs):
    n_out = len(refs) // 2
    w_refs, o_refs = refs[:n_out], refs[n_out:]
    n = _rmsnorm(x_ref[...], g_ref[...]).astype(BF16)
    for w_ref, o_ref in zip(w_refs, o_refs):
        cols = w_ref.shape[1]
        for c0 in range(0, cols, COL_CHUNK):
            c1 = min(c0 + COL_CHUNK, cols)
            o_ref[:, c0:c1] = jnp.dot(n, w_ref[:, c0:c1], preferred_element_type=F32).astype(o_ref.dtype)


def _norm_matmul(x, g, ws, name):
    rows, k = x.shape
    tm = min(TOKEN_TILE, rows)
    return pl.pallas_call(
        _norm_matmul_kernel,
        grid=(rows // tm,),
        in_specs=[pl.BlockSpec((tm, k), lambda i: (i, 0)), _const_spec((1, k))]
        + [_const_spec(w.shape) for w in ws],
        out_specs=[pl.BlockSpec((tm, w.shape[1]), lambda i: (i, 0)) for w in ws],
        out_shape=[jax.ShapeDtypeStruct((rows, w.shape[1]), F32) for w in ws],
        compiler_params=_params(("parallel",)),
        name=name,
    )(x, g, *ws)


def _hgrn_kernel(p_ref, s0_ref, lb_ref, gn_ref, o_ref, sn_ref, st_scr, *, chunk):
    n_blk = chunk // HG_BLOCK
    W = HG_WIDTH
    heads = [slice(h * HG_DIM, (h + 1) * HG_DIM) for h in range(HG_HEADS)]
    hr = range(HG_HEADS)

    @pl.when(pl.program_id(1) == 0)
    def _():
        for h in hr:
            st_scr[h] = s0_ref[0, h].T

    ti = lax.broadcasted_iota(jnp.int32, (chunk, chunk), 0)
    tj = lax.broadcasted_iota(jnp.int32, (chunk, chunk), 1)
    diag_mask = (ti >= tj) & (ti // HG_BLOCK == tj // HG_BLOCK)
    ri = lax.broadcasted_iota(jnp.int32, (3 * chunk, 3 * chunk), 0)
    ci = lax.broadcasted_iota(jnp.int32, (3 * chunk, 3 * chunk), 1) % chunk
    kind, rt = ri // chunk, ri % chunk
    same = rt // HG_BLOCK == ci // HG_BLOCK
    as_f32 = lambda m: m.astype(F32)
    cum_mask = jnp.where(kind == 0, as_f32(rt >= ci),
                         jnp.where(kind == 1, as_f32((rt >= ci) & same),
                                   as_f32(ci // HG_BLOCK <= rt // HG_BLOCK))).astype(BF16)
    n_chunks = p_ref.shape[1] // chunk
    group = HG_CHUNK_GROUP if n_chunks % HG_CHUNK_GROUP == 0 else 1
    R = group * chunk
    cr = range(group)
    units = [(c, h) for c in cr for h in hr]
    ur = range(len(units))
    lb = lb_ref[...]
    gn = gn_ref[...]

    def rows(x, c):
        return x[c * chunk:(c + 1) * chunk]

    def part(x, unit):
        c, h = unit
        return x[c * chunk:(c + 1) * chunk, heads[h]]

    def one_group(i, carry):
        off = pl.multiple_of(i * R, R)
        p = p_ref[0, pl.ds(off, R), :]
        q, fpre, v, gate = p[:, :W], p[:, W:2 * W], p[:, 2 * W:3 * W], p[:, 3 * W:]
        sig = jax.nn.sigmoid(fpre)
        logf = jnp.log(lb + (1.0 - lb) * sig) * LOG2_E
        k = (1.0 - lb) * (1.0 - sig)
        cums = [_cumdot(cum_mask, rows(logf, c)) for c in cr]
        stack = lambda pieces: jnp.concatenate(pieces, axis=0) if len(pieces) > 1 else pieces[0]
        lc = stack([cums[c][:chunk] for c in cr])
        lblk = stack([cums[c][chunk:2 * chunk] for c in cr])
        lend = stack([cums[c][2 * chunk:] for c in cr])
        llast = [cums[c][chunk - 1:chunk, :] for c in cr]
        to_end = stack([llast[c] - cums[c][:chunk] for c in cr])

        q_in = q * jnp.exp2(lc)
        qg = q * jnp.exp2(lblk)
        kg = k * jnp.exp2(-lblk)
        kd = (k * jnp.exp2(lend - lc)).astype(BF16)
        k_end = k * jnp.exp2(to_end)
        g_chunk = [jnp.exp2(llast[c]) for c in cr]
        q_cross, k_cross = [], []
        for j in range(n_blk - 1):
            lo, hi = j * HG_BLOCK, (j + 1) * HG_BLOCK
            q_rows, k_rows = [], []
            for c in cr:
                end_j = cums[c][2 * chunk + lo:2 * chunk + lo + 1, :]
                later = slice(c * chunk + hi, (c + 1) * chunk)
                q_rows += [jnp.zeros((hi, W), BF16), (q[later] * jnp.exp2(lc[later] - end_j)).astype(BF16)]
                k_rows += [jnp.zeros((lo, W), BF16)] if lo else []
                k_rows += [kd[c * chunk + lo:c * chunk + hi], jnp.zeros((chunk - hi, W), BF16)]
            q_cross.append(jnp.concatenate(q_rows, axis=0))
            k_cross.append(jnp.concatenate(k_rows, axis=0))

        att = [jnp.where(diag_mask, _bdot(part(qg, u), part(kg, u), NT), 0.0) for u in units]
        if n_blk > 1:
            att = [att[n] + _bdot(jnp.concatenate([part(qc, units[n]) for qc in q_cross], axis=1),
                                  jnp.concatenate([part(kc, units[n]) for kc in k_cross], axis=1), NT) for n in ur]
        o_intra = [_bdot(att[n], part(v, units[n])) for n in ur]
        kv = [_bdot(part(v, u), part(k_end, u), TN) for u in units]
        st = [st_scr[h] for h in hr]
        for c in cr:
            base = c * HG_HEADS
            o = [_bdot(part(q_in, (c, h)), st[h], NT) + o_intra[base + h] for h in hr]
            st = [st[h] * g_chunk[c][:, heads[h]] + kv[base + h] for h in hr]
            for h in hr:
                on = o[h] * lax.rsqrt(jnp.mean(o[h] * o[h], axis=-1, keepdims=True) + RMS_EPS)
                o_ref[0, pl.ds(off + c * chunk, chunk), heads[h]] = (
                    on * gn[:, heads[h]] * jax.nn.silu(part(gate, (c, h)))).astype(o_ref.dtype)
        for h in hr:
            st_scr[h] = st[h]
        return carry

    lax.fori_loop(0, n_chunks // group, one_group, 0)

    @pl.when(pl.program_id(1) == pl.num_programs(1) - 1)
    def _():
        for h in hr:
            sn_ref[0, h] = st_scr[h].T


def _hgrn(p, s0, lb, gn):
    bsz, t_len, _ = p.shape
    chunk = min(TIME_CHUNK, t_len)
    t_blk = min(TIME_BLOCK, t_len)
    return pl.pallas_call(
        functools.partial(_hgrn_kernel, chunk=chunk),
        grid=(bsz, t_len // t_blk),
        in_specs=[pl.BlockSpec((1, t_blk, HG_COLS), lambda b, c: (b, c, 0)),
                  pl.BlockSpec((1, HG_HEADS, HG_DIM, HG_DIM), lambda b, c: (b, 0, 0, 0)),
                  _const_spec((1, HG_WIDTH)), _const_spec((1, HG_WIDTH))],
        out_specs=[pl.BlockSpec((1, t_blk, HG_WIDTH), lambda b, c: (b, c, 0)),
                   pl.BlockSpec((1, HG_HEADS, HG_DIM, HG_DIM), lambda b, c: (b, 0, 0, 0))],
        out_shape=[jax.ShapeDtypeStruct((bsz, t_len, HG_WIDTH), BF16),
                   jax.ShapeDtypeStruct((bsz, HG_HEADS, HG_DIM, HG_DIM), F32)],
        scratch_shapes=[pltpu.VMEM((HG_HEADS, HG_DIM, HG_DIM), F32)],
        compiler_params=_params(("parallel", "arbitrary")),
        name="hgrn",
    )(p, s0, lb, gn)


def _rwkv_kernel(p_ref, s0_ref, sh0_ref, mu_ref, w0_ref, wb_ref, a0_ref, ab_ref, gb_ref, kk_ref, ka_ref,
                 rk_ref, gnw_ref, gnb_ref, o_ref, sn_ref, shn_ref, s_scr, sh_scr, *, chunk):
    W, N, C = RW_WIDTH, RW_HEAD, chunk
    heads = [slice(h * N, (h + 1) * N) for h in range(RW_HEADS)]
    hr = range(RW_HEADS)

    @pl.when(pl.program_id(1) == 0)
    def _():
        s_scr[...] = s0_ref[0]
        sh_scr[...] = sh0_ref[0]

    n_chunks = p_ref.shape[1] // C
    group = RW_CHUNK_GROUP if n_chunks % RW_CHUNK_GROUP == 0 else 1
    R = group * C
    cr = range(group)
    units = [(ci, h) for ci in cr for h in hr]
    ur = range(len(units))
    row = lax.broadcasted_iota(jnp.int32, (R, 1), 0)
    ti = lax.broadcasted_iota(jnp.int32, (C, C), 0)
    tj = lax.broadcasted_iota(jnp.int32, (C, C), 1)
    incl = ti >= tj
    strict = ti > tj
    mi = lax.broadcasted_iota(jnp.int32, (R, 3 * R), 0)
    mj = lax.broadcasted_iota(jnp.int32, (R, 3 * R), 1) % R
    cum_mask = ((mi >= mj) & (mi // C == mj // C)).astype(BF16)

    def part(x, unit):
        ci, h = unit
        return x[ci * C:(ci + 1) * C, heads[h]]

    def one_group(i, carry):
        off = pl.multiple_of(i * R, R)
        p = p_ref[0, pl.ds(off, R), :]
        prev = jnp.where(row == 0, sh_scr[...], pltpu.roll(p, 1, 0))
        sh_scr[...] = p[R - 1:R, :]
        xs = p + (prev - p) * mu_ref[...]
        r, k, v = xs[:, :W], xs[:, W:2 * W], xs[:, 2 * W:3 * W]
        wd = xs[:, 3 * W:3 * W + RW_DECAY_LORA]
        ad = xs[:, 3 * W + RW_DECAY_LORA:3 * W + RW_DECAY_LORA + RW_A_LORA]
        gd = xs[:, 3 * W + RW_DECAY_LORA + RW_A_LORA:]

        w_log = -_softplus(-(w0_ref[...] + _bdot(jnp.tanh(wd), wb_ref[...]))) - 0.5
        ld = -jnp.exp(w_log) * LOG2_E
        a_icl = jax.nn.sigmoid(a0_ref[...] + _bdot(ad, ab_ref[...]))
        gate = _bdot(jax.nn.sigmoid(gd), gb_ref[...])
        kk = k * kk_ref[...]
        kk = kk / jnp.maximum(jnp.sqrt(_head_sums(kk * kk)), 1e-12)
        k2 = k * (1.0 + (a_icl - 1.0) * ka_ref[...])
        av = -kk
        bv = kk * a_icl

        lc = _cumdot(cum_mask, ld)
        llast = [lc[ci * C + C - 1:ci * C + C, :] for ci in cr]
        to_end = jnp.concatenate([llast[ci] - lc[ci * C:(ci + 1) * C] for ci in cr], axis=0)
        e_in = jnp.exp2(lc)
        e_inv = jnp.exp2(-lc)
        e_end = jnp.exp2(to_end)
        a_t = av * jnp.exp2(lc - ld)
        r_t = r * e_in
        b_t = bv * e_inv
        k_t = k2 * e_inv
        b_h = bv * e_end
        k_h = k2 * e_end
        g_chunk = [jnp.exp2(llast[ci]) for ci in cr]

        g = [_bdot(jnp.concatenate([part(a_t, u), part(r_t, u)], axis=0),
                   jnp.concatenate([part(b_t, u), part(k_t, u)], axis=0), NT) for u in units]
        l_pow = [jnp.where(strict, g[n][:C, :C], 0.0).astype(BF16) for n in ur]
        a_rb = [jnp.where(incl, g[n][C:, :C], 0.0).astype(BF16) for n in ur]
        vh = [part(v, u).astype(BF16) for u in units]
        kv = [_bdot(jnp.concatenate([jnp.where(strict, g[n][:C, C:], 0.0),
                                     jnp.where(incl, g[n][C:, C:], 0.0)], axis=0), vh[n]) for n in ur]
        au = [jnp.concatenate([part(a_t, units[n]), kv[n][:C]], axis=1) for n in ur]
        span = 1
        while 2 * span < C:
            res = [_bdot(l_pow[n], jnp.concatenate([au[n].astype(BF16), l_pow[n]], axis=1)) for n in ur]
            au = [au[n] + res[n][:, :2 * N] for n in ur]
            l_pow = [res[n][:, 2 * N:].astype(BF16) for n in ur]
            span *= 2
        au = [au[n] + _bdot(l_pow[n], au[n]) for n in ur]
        mn = [_bdot(au[n], part(b_h, units[n]), TN) for n in ur]
        nt = [mn[n][N:] + _bdot(vh[n], part(k_h, units[n]), TN) for n in ur]
        ry = [_bdot(a_rb[n], au[n]) for n in ur]
        y0 = [ry[n][:, N:] + kv[n][C:] for n in ur]
        r_hat = [part(r_t, units[n]) + ry[n][:, :N] for n in ur]
        s_cur = [s_scr[h] for h in hr]
        ys = []
        for ci in cr:
            base = ci * RW_HEADS
            ys.append([_bdot(r_hat[base + h], s_cur[h], NT) + y0[base + h] for h in hr])
            s_cur = [s_cur[h] * g_chunk[ci][:, heads[h]] + _bdot(s_cur[h], mn[base + h][:N]) + nt[base + h]
                     for h in hr]
        for h in hr:
            s_scr[h] = s_cur[h]

        y = jnp.concatenate([jnp.concatenate(ys[ci], axis=1) for ci in cr], axis=0)
        mean = _head_sums(y) * (1.0 / N)
        yc = y - mean
        var = _head_sums(yc * yc) * (1.0 / N)
        yn = yc * lax.rsqrt(var + GN_EPS) * gnw_ref[...] + gnb_ref[...]
        bonus = _head_sums(r * k2 * rk_ref[...]) * v
        o_ref[0, pl.ds(off, R), :] = ((yn + bonus) * gate).astype(o_ref.dtype)
        return carry

    lax.fori_loop(0, n_chunks // group, one_group, 0)

    @pl.when(pl.program_id(1) == pl.num_programs(1) - 1)
    def _():
        sn_ref[0] = s_scr[...]
        shn_ref[0] = sh_scr[...]


def _rwkv(p, s0, sh0, vecs, wb, ab, gb):
    bsz, t_len, _ = p.shape
    chunk = min(TIME_CHUNK, t_len)
    t_blk = min(TIME_BLOCK, t_len)
    mu, w0, a0, kk, ka, rk, gnw, gnb = vecs
    state_spec = pl.BlockSpec((1, RW_HEADS, RW_HEAD, RW_HEAD), lambda b, c: (b, 0, 0, 0))
    shift_spec = pl.BlockSpec((1, 1, RW_COLS), lambda b, c: (b, 0, 0))
    return pl.pallas_call(
        functools.partial(_rwkv_kernel, chunk=chunk),
        grid=(bsz, t_len // t_blk),
        in_specs=[pl.BlockSpec((1, t_blk, RW_COLS), lambda b, c: (b, c, 0)), state_spec, shift_spec,
                  _const_spec(mu.shape), _const_spec(w0.shape), _const_spec(wb.shape), _const_spec(a0.shape),
                  _const_spec(ab.shape), _const_spec(gb.shape), _const_spec(kk.shape), _const_spec(ka.shape),
                  _const_spec(rk.shape), _const_spec(gnw.shape), _const_spec(gnb.shape)],
        out_specs=[pl.BlockSpec((1, t_blk, RW_WIDTH), lambda b, c: (b, c, 0)), state_spec, shift_spec],
        out_shape=[jax.ShapeDtypeStruct((bsz, t_len, RW_WIDTH), BF16),
                   jax.ShapeDtypeStruct((bsz, RW_HEADS, RW_HEAD, RW_HEAD), F32),
                   jax.ShapeDtypeStruct((bsz, 1, RW_COLS), F32)],
        scratch_shapes=[pltpu.VMEM((RW_HEADS, RW_HEAD, RW_HEAD), F32), pltpu.VMEM((1, RW_COLS), F32)],
        compiler_params=_params(("parallel", "arbitrary")),
        name="rwkv",
    )(p, s0, sh0, mu, w0, wb, a0, ab, gb, kk, ka, rk, gnw, gnb)


def _cross_kernel(x_ref, a_ref, b_ref, k_ref, v_ref, wa_ref, wb_ref, g_ref, wq_ref, wo_ref, o_ref):
    bb, tq, d = x_ref.shape
    rows = bb * tq
    x = x_ref[...].reshape(rows, d)
    x = x + jnp.dot(a_ref[...].reshape(rows, a_ref.shape[2]), wa_ref[...], preferred_element_type=F32)
    x = x + jnp.dot(b_ref[...].reshape(rows, b_ref.shape[2]), wb_ref[...], preferred_element_type=F32)
    q = jnp.dot(_rmsnorm(x, g_ref[...]).astype(BF16), wq_ref[...], preferred_element_type=F32)
    heads = [slice(h * X_DIM, (h + 1) * X_DIM) for h in range(X_HEADS)]
    units = [(i, sl) for i in range(bb) for sl in heads]
    s = [_bdot(q[i * tq:(i + 1) * tq, sl], k_ref[i, :, sl], NT) * (X_DIM ** -0.5) for i, sl in units]
    e = [jnp.exp(sn - jnp.max(sn, axis=-1, keepdims=True)) for sn in s]
    pr = [en / jnp.sum(en, axis=-1, keepdims=True) for en in e]
    o = [_bdot(pr[n], v_ref[i, :, sl]).astype(BF16) for n, (i, sl) in enumerate(units)]
    per_row = [jnp.concatenate(o[i * X_HEADS:(i + 1) * X_HEADS], axis=1) for i in range(bb)]
    attn = jnp.concatenate(per_row, axis=0) if bb > 1 else per_row[0]
    out = x + jnp.dot(attn, wo_ref[...], preferred_element_type=F32)
    o_ref[...] = out.reshape(bb, tq, d)


def _cross(x, a, b, mk, mv, wa, wb, g, wq, wo):
    bsz, t_len, d = x.shape
    tq = min(CROSS_TILE, t_len)
    bb = min(bsz, TOKEN_TILE // min(tq, TOKEN_TILE))
    tok_spec = lambda width: pl.BlockSpec((bb, tq, width), lambda i, t: (i, t, 0))
    mem_spec = pl.BlockSpec((bb, N_MEM, d), lambda i, t: (i, 0, 0))
    return pl.pallas_call(
        _cross_kernel,
        grid=(bsz // bb, t_len // tq),
        in_specs=[tok_spec(d), tok_spec(a.shape[2]), tok_spec(b.shape[2]), mem_spec, mem_spec,
                  _const_spec(wa.shape), _const_spec(wb.shape), _const_spec((1, d)),
                  _const_spec(wq.shape), _const_spec(wo.shape)],
        out_specs=tok_spec(d),
        out_shape=jax.ShapeDtypeStruct(x.shape, F32),
        compiler_params=_params(("parallel", "parallel")),
        name="cross",
    )(x, a, b, mk, mv, wa, wb, g, wq, wo)


def _ffn_kernel(x_ref, g_ref, w1_ref, w3_ref, w2_ref, gf_ref, o_ref):
    x = x_ref[...]
    n = _rmsnorm(x, g_ref[...]).astype(BF16)
    acc = x
    for c0 in range(0, D_FF, FF_CHUNK):
        c1 = c0 + FF_CHUNK
        up = jnp.dot(n, w1_ref[:, c0:c1], preferred_element_type=F32)
        lin = jnp.dot(n, w3_ref[:, c0:c1], preferred_element_type=F32)
        hid = (jax.nn.silu(up) * lin).astype(BF16)
        acc = acc + jnp.dot(hid, w2_ref[c0:c1, :], preferred_element_type=F32)
    o_ref[...] = _rmsnorm(acc, gf_ref[...])


def _ffn(x, g, w1, w3, w2, gf):
    rows, d = x.shape
    tm = min(FFN_TILE, rows)
    row_spec = pl.BlockSpec((tm, d), lambda i: (i, 0))
    return pl.pallas_call(
        _ffn_kernel,
        grid=(rows // tm,),
        in_specs=[row_spec, _const_spec((1, d)), _const_spec(w1.shape), _const_spec(w3.shape),
                  _const_spec(w2.shape), _const_spec((1, d))],
        out_specs=row_spec,
        out_shape=jax.ShapeDtypeStruct((rows, d), F32),
        compiler_params=_params(("parallel",)),
        name="ffn",
    )(x, g, w1, w3, w2, gf)


def _trunk(x, mk, mv, hg0, rw0, sh0, lb, w):
    bsz, t_len, d = x.shape
    x2d = x.reshape(bsz * t_len, d)
    p_hg, p_rw = _norm_matmul(x2d, w['norm_mix'], [w['w_in_hg'], w['w_in_rw']], "in_proj")
    hg_o, hg_s = _hgrn(p_hg.reshape(bsz, t_len, HG_COLS), hg0, lb, w['hgrn_norm'])
    rw_o, rw_s, rw_sh = _rwkv(p_rw.reshape(bsz, t_len, RW_COLS), rw0, sh0, w['rw_vecs'],
                              w['rw_w_b'], w['rw_a_b'], w['rw_g_b'])
    x3d = _cross(x, hg_o, rw_o, mk, mv, w['w_out_hg'], w['w_out_rw'], w['norm_cross'], w['w_cq'], w['w_co'])
    y = _ffn(x3d.reshape(bsz * t_len, d), w['norm_ffn'], w['w_ff1'], w['w_ff3'], w['w_ff2'], w['norm_final'])
    return y.reshape(bsz, t_len, d), hg_s, rw_s, rw_sh


def kernel(x_prompt, mem_prompt, x_sample, cache_mem_k, cache_mem_v, state_hgrn, state_rwkv, state_rwkv_shift, hgrn_lb_logits, norm_mix, w_in, hgrn_norm, rw_mu, rw_w0, rw_w_b, rw_a0, rw_a_b, rw_g_b, rw_k_k, rw_k_a, rw_r_k, rw_gn_w, rw_gn_b, w_out, norm_cross, norm_mem, w_cq, w_ck, w_cv, w_co, norm_ffn, w_ff1, w_ff3, w_ff2, norm_final):
    assert w_in.shape[0] == 1, "single-layer trunk"
    bp = x_prompt.shape[0]
    row = lambda a: a.reshape(1, -1).astype(F32)
    bf = lambda a: a.astype(BF16)
    lb = jnp.cumsum(jax.nn.softmax(hgrn_lb_logits.astype(F32), axis=0), axis=0)[0:1]
    w = {
        'norm_mix': row(norm_mix[0]), 'w_in_hg': bf(w_in[0, :, :HG_COLS]), 'w_in_rw': bf(w_in[0, :, HG_COLS:]),
        'hgrn_norm': row(hgrn_norm[0]),
        'rw_vecs': tuple(row(a[0]) for a in (rw_mu, rw_w0, rw_a0, rw_k_k, rw_k_a, rw_r_k, rw_gn_w, rw_gn_b)),
        'rw_w_b': bf(rw_w_b[0]), 'rw_a_b': bf(rw_a_b[0]), 'rw_g_b': bf(rw_g_b[0]),
        'w_out_hg': bf(w_out[0, :HG_WIDTH]), 'w_out_rw': bf(w_out[0, HG_WIDTH:]),
        'norm_cross': row(norm_cross[0]), 'w_cq': bf(w_cq[0]), 'w_co': bf(w_co[0]),
        'norm_ffn': row(norm_ffn[0]), 'w_ff1': bf(w_ff1[0]), 'w_ff3': bf(w_ff3[0]), 'w_ff2': bf(w_ff2[0]),
        'norm_final': row(norm_final),
    }
    mk, mv = _norm_matmul(mem_prompt.reshape(bp * N_MEM, D_MODEL), row(norm_mem[0]),
                          [bf(w_ck[0]), bf(w_cv[0])], "mem_kv")
    mk = mk.reshape(bp, N_MEM, D_MODEL)
    mv = mv.reshape(bp, N_MEM, D_MODEL)
    y_p, p_hg, p_rw, p_sh = _trunk(
        x_prompt, mk, mv, jnp.zeros((bp, HG_HEADS, HG_DIM, HG_DIM), F32),
        jnp.zeros((bp, RW_HEADS, RW_HEAD, RW_HEAD), F32), jnp.zeros((bp, 1, RW_COLS), F32), lb, w)
    bs = x_sample.shape[0]
    y_s, s_hg, s_rw, s_sh = _trunk(
        x_sample, cache_mem_k[0].reshape(bs, N_MEM, D_MODEL), cache_mem_v[0].reshape(bs, N_MEM, D_MODEL),
        state_hgrn[0], state_rwkv[0], state_rwkv_shift[0], lb, w)
    return (y_p, y_s, p_hg[None], p_rw[None], p_sh[None],
            mk.reshape(1, bp, N_MEM, X_HEADS, X_DIM), mv.reshape(1, bp, N_MEM, X_HEADS, X_DIM),
            s_hg[None], s_rw[None], s_sh[None])
```

```python
import functools

import jax
import jax.numpy as jnp
from jax import lax
from jax.experimental import pallas as pl
from jax.experimental.pallas import tpu as pltpu

F32 = jnp.float32
BF16 = jnp.bfloat16

D_MODEL = 1024
HG_WIDTH = 512
HG_HEADS = 4
HG_DIM = 128
HG_BLOCK = 16
HG_COLS = 4 * HG_WIDTH
RW_WIDTH = 512
RW_HEAD = 64
RW_HEADS = 8
RW_DECAY_LORA = 64
RW_A_LORA = 64
RW_GATE_LORA = 128
RW_COLS = 3 * RW_WIDTH + RW_DECAY_LORA + RW_A_LORA + RW_GATE_LORA
N_MEM = 256
X_HEADS = 4
X_DIM = D_MODEL // X_HEADS
D_FF = 2816
RMS_EPS = 1e-6
GN_EPS = 64e-5
LOG2_E = 1.4426950408889634

TIME_CHUNK = 64
TIME_BLOCK = 1024
HG_CHUNK_GROUP = 8
RW_CHUNK_GROUP = 4
TOKEN_TILE = 512
CROSS_TILE = 1024
FFN_TILE = 1024
FF_CHUNK = 256
COL_CHUNK = 1024
VMEM_LIMIT_BYTES = 56 * 1024 * 1024

NN = (((1,), (0,)), ((), ()))
NT = (((1,), (1,)), ((), ()))
TN = (((0,), (0,)), ((), ()))


def _bdot(a, b, dims=NN):
    return lax.dot_general(a.astype(BF16), b.astype(BF16), dims, preferred_element_type=F32)


def _split3(x):
    hi = x.astype(BF16)
    r1 = x - hi.astype(F32)
    mid = r1.astype(BF16)
    lo = (r1 - mid.astype(F32)).astype(BF16)
    return hi, mid, lo


def _cumdot(mask3, x):
    return jnp.dot(mask3, jnp.concatenate(_split3(x), axis=0), preferred_element_type=F32)


def _head_sums(x):
    pair = 2 * RW_HEAD
    low = lax.broadcasted_iota(jnp.int32, (x.shape[0], pair), 1) < RW_HEAD
    out = []
    for c0 in range(0, x.shape[1], pair):
        xp = x[:, c0:c0 + pair]
        s_low = jnp.sum(jnp.where(low, xp, 0.0), axis=1, keepdims=True)
        s_high = jnp.sum(jnp.where(low, 0.0, xp), axis=1, keepdims=True)
        out.append(jnp.where(low, s_low, s_high))
    return jnp.concatenate(out, axis=1)


def _rmsnorm(x, g):
    return x * lax.rsqrt(jnp.mean(x * x, axis=-1, keepdims=True) + RMS_EPS) * g


def _softplus(x):
    return jnp.maximum(x, 0.0) + jnp.log(1.0 + jnp.exp(-jnp.abs(x)))


def _params(semantics):
    return pltpu.CompilerParams(dimension_semantics=semantics, vmem_limit_bytes=VMEM_LIMIT_BYTES)


def _const_spec(shape):
    zeros = (0,) * len(shape)
    return pl.BlockSpec(shape, lambda *_: zeros)


def _rows_per_step(bsz, chunks_per_row, group):
    if chunks_per_row >= group:
        return 1
    rows = max(1, group // chunks_per_row)
    while bsz % rows:
        rows -= 1
    return rows


def _norm_matmul_kernel(x_ref, g_ref, *refs):
    n_out = len(refs) // 2
    w_refs, o_refs = refs[:n_out], refs[n_out:]
    n = _rmsnorm(x_ref[...], g_ref[...]).astype(BF16)
    for w_ref, o_ref in zip(w_refs, o_refs):
        cols = w_ref.shape[1]
        for c0 in range(0, cols, COL_CHUNK):
            c1 = min(c0 + COL_CHUNK, cols)
            o_ref[:, c0:c1] = jnp.dot(n, w_ref[:, c0:c1], preferred_element_type=F32).astype(o_ref.dtype)


def _norm_matmul(x, g, ws, name):
    rows, k = x.shape
    tm = min(TOKEN_TILE, rows)
    return pl.pallas_call(
        _norm_matmul_kernel,
        grid=(rows // tm,),
        in_specs=[pl.BlockSpec((tm, k), lambda i: (i, 0)), _const_spec((1, k))]
        + [_const_spec(w.shape) for w in ws],
        out_specs=[pl.BlockSpec((tm, w.shape[1]), lambda i: (i, 0)) for w in ws],
        out_shape=[jax.ShapeDtypeStruct((rows, w.shape[1]), F32) for w in ws],
        compiler_params=_params(("parallel",)),
        name=name,
    )(x, g, *ws)


def _hgrn_kernel(p_ref, s0_ref, lb_ref, gn_ref, o_ref, sn_ref, st_scr, *, chunk):
    n_blk = chunk // HG_BLOCK
    W = HG_WIDTH
    heads = [slice(h * HG_DIM, (h + 1) * HG_DIM) for h in range(HG_HEADS)]
    hr = range(HG_HEADS)

    @pl.when(pl.program_id(1) == 0)
    def _():
        for bi in range(p_ref.shape[0]):
            for h in hr:
                st_scr[bi, h] = s0_ref[bi, h].T

    ti = lax.broadcasted_iota(jnp.int32, (chunk, chunk), 0)
    tj = lax.broadcasted_iota(jnp.int32, (chunk, chunk), 1)
    diag_mask = (ti >= tj) & (ti // HG_BLOCK == tj // HG_BLOCK)
    ri = lax.broadcasted_iota(jnp.int32, (3 * chunk, 3 * chunk), 0)
    ci = lax.broadcasted_iota(jnp.int32, (3 * chunk, 3 * chunk), 1) % chunk
    kind, rt = ri // chunk, ri % chunk
    same = rt // HG_BLOCK == ci // HG_BLOCK
    as_f32 = lambda m: m.astype(F32)
    cum_mask = jnp.where(kind == 0, as_f32(rt >= ci),
                         jnp.where(kind == 1, as_f32((rt >= ci) & same),
                                   as_f32(ci // HG_BLOCK <= rt // HG_BLOCK))).astype(BF16)
    n_rows = p_ref.shape[0]
    n_chunks = p_ref.shape[1] // chunk
    group = HG_CHUNK_GROUP if n_chunks % HG_CHUNK_GROUP == 0 else 1
    R = group * chunk
    cr = range(n_rows * group)
    units = [(c, h) for c in cr for h in hr]
    ur = range(len(units))
    lb = lb_ref[...]
    gn = gn_ref[...]

    def rows(x, c):
        return x[c * chunk:(c + 1) * chunk]

    def part(x, unit):
        c, h = unit
        return x[c * chunk:(c + 1) * chunk, heads[h]]

    def one_group(i, carry):
        off = pl.multiple_of(i * R, R)
        p = [p_ref[bi, pl.ds(off, R), :] for bi in range(n_rows)]
        p = jnp.concatenate(p, axis=0) if n_rows > 1 else p[0]
        q, fpre, v, gate = p[:, :W], p[:, W:2 * W], p[:, 2 * W:3 * W], p[:, 3 * W:]
        sig = jax.nn.sigmoid(fpre)
        logf = jnp.log(lb + (1.0 - lb) * sig) * LOG2_E
        k = (1.0 - lb) * (1.0 - sig)
        cums = [_cumdot(cum_mask, rows(logf, c)) for c in cr]
        stack = lambda pieces: jnp.concatenate(pieces, axis=0) if len(pieces) > 1 else pieces[0]
        lc = stack([cums[c][:chunk] for c in cr])
        lblk = stack([cums[c][chunk:2 * chunk] for c in cr])
        lend = stack([cums[c][2 * chunk:] for c in cr])
        llast = [cums[c][chunk - 1:chunk, :] for c in cr]
        to_end = stack([llast[c] - cums[c][:chunk] for c in cr])

        q_in = q * jnp.exp2(lc)
        qg = q * jnp.exp2(lblk)
        kg = k * jnp.exp2(-lblk)
        kd = (k * jnp.exp2(lend - lc)).astype(BF16)
        k_end = k * jnp.exp2(to_end)
        g_chunk = [jnp.exp2(llast[c]) for c in cr]
        q_cross, k_cross = [], []
        for j in range(n_blk - 1):
            lo, hi = j * HG_BLOCK, (j + 1) * HG_BLOCK
            q_rows, k_rows = [], []
            for c in cr:
                end_j = cums[c][2 * chunk + lo:2 * chunk + lo + 1, :]
                later = slice(c * chunk + hi, (c + 1) * chunk)
                q_rows += [jnp.zeros((hi, W), BF16), (q[later] * jnp.exp2(lc[later] - end_j)).astype(BF16)]
                k_rows += [jnp.zeros((lo, W), BF16)] if lo else []
                k_rows += [kd[c * chunk + lo:c * chunk + hi], jnp.zeros((chunk - hi, W), BF16)]
            q_cross.append(jnp.concatenate(q_rows, axis=0))
            k_cross.append(jnp.concatenate(k_rows, axis=0))

        att = [jnp.where(diag_mask, _bdot(part(qg, u), part(kg, u), NT), 0.0) for u in units]
        if n_blk > 1:
            att = [att[n] + _bdot(jnp.concatenate([part(qc, units[n]) for qc in q_cross], axis=1),
                                  jnp.concatenate([part(kc, units[n]) for kc in k_cross], axis=1), NT) for n in ur]
        o_intra = [_bdot(att[n], part(v, units[n])) for n in ur]
        kv = [_bdot(part(v, u), part(k_end, u), TN) for u in units]
        for bi in range(n_rows):
            st = [st_scr[bi, h] for h in hr]
            for c in range(bi * group, (bi + 1) * group):
                base = c * HG_HEADS
                o = [_bdot(part(q_in, (c, h)), st[h], NT) + o_intra[base + h] for h in hr]
                st = [st[h] * g_chunk[c][:, heads[h]] + kv[base + h] for h in hr]
                for h in hr:
                    on = o[h] * lax.rsqrt(jnp.mean(o[h] * o[h], axis=-1, keepdims=True) + RMS_EPS)
                    o_ref[bi, pl.ds(off + (c - bi * group) * chunk, chunk), heads[h]] = (
                        on * gn[:, heads[h]] * jax.nn.silu(part(gate, (c, h)))).astype(o_ref.dtype)
            for h in hr:
                st_scr[bi, h] = st[h]
        return carry

    lax.fori_loop(0, n_chunks // group, one_group, 0)

    @pl.when(pl.program_id(1) == pl.num_programs(1) - 1)
    def _():
        for bi in range(n_rows):
            for h in hr:
                sn_ref[bi, h] = st_scr[bi, h].T


def _hgrn(p, s0, lb, gn):
    bsz, t_len, _ = p.shape
    chunk = min(TIME_CHUNK, t_len)
    t_blk = min(TIME_BLOCK, t_len)
    bb = _rows_per_step(bsz, t_len // chunk, HG_CHUNK_GROUP)
    state_spec = pl.BlockSpec((bb, HG_HEADS, HG_DIM, HG_DIM), lambda b, c: (b, 0, 0, 0))
    return pl.pallas_call(
        functools.partial(_hgrn_kernel, chunk=chunk),
        grid=(bsz // bb, t_len // t_blk),
        in_specs=[pl.BlockSpec((bb, t_blk, HG_COLS), lambda b, c: (b, c, 0)), state_spec,
                  _const_spec((1, HG_WIDTH)), _const_spec((1, HG_WIDTH))],
        out_specs=[pl.BlockSpec((bb, t_blk, HG_WIDTH), lambda b, c: (b, c, 0)), state_spec],
        out_shape=[jax.ShapeDtypeStruct((bsz, t_len, HG_WIDTH), BF16),
                   jax.ShapeDtypeStruct((bsz, HG_HEADS, HG_DIM, HG_DIM), F32)],
        scratch_shapes=[pltpu.VMEM((bb, HG_HEADS, HG_DIM, HG_DIM), F32)],
        compiler_params=_params(("parallel", "arbitrary")),
        name="hgrn",
    )(p, s0, lb, gn)


def _rwkv_kernel(p_ref, s0_ref, sh0_ref, mu_ref, w0_ref, wb_ref, a0_ref, ab_ref, gb_ref, kk_ref, ka_ref,
                 rk_ref, gnw_ref, gnb_ref, o_ref, sn_ref, shn_ref, s_scr, sh_scr, *, chunk):
    W, N, C = RW_WIDTH, RW_HEAD, chunk
    heads = [slice(h * N, (h + 1) * N) for h in range(RW_HEADS)]
    hr = range(RW_HEADS)

    @pl.when(pl.program_id(1) == 0)
    def _():
        s_scr[...] = s0_ref[...]
        sh_scr[...] = sh0_ref[...]

    n_rows = p_ref.shape[0]
    n_chunks = p_ref.shape[1] // C
    group = RW_CHUNK_GROUP if n_chunks % RW_CHUNK_GROUP == 0 else 1
    R = group * C
    RR = n_rows * R
    cr = range(n_rows * group)
    units = [(ci, h) for ci in cr for h in hr]
    ur = range(len(units))
    row = lax.broadcasted_iota(jnp.int32, (R, 1), 0)
    ti = lax.broadcasted_iota(jnp.int32, (C, C), 0)
    tj = lax.broadcasted_iota(jnp.int32, (C, C), 1)
    incl = ti >= tj
    strict = ti > tj
    mi = lax.broadcasted_iota(jnp.int32, (RR, 3 * RR), 0)
    mj = lax.broadcasted_iota(jnp.int32, (RR, 3 * RR), 1) % RR
    cum_mask = ((mi >= mj) & (mi // C == mj // C)).astype(BF16)

    def part(x, unit):
        ci, h = unit
        return x[ci * C:(ci + 1) * C, heads[h]]

    def one_group(i, carry):
        off = pl.multiple_of(i * R, R)
        xs = []
        for bi in range(n_rows):
            p = p_ref[bi, pl.ds(off, R), :]
            prev = jnp.where(row == 0, sh_scr[bi], pltpu.roll(p, 1, 0))
            sh_scr[bi] = p[R - 1:R, :]
            xs.append(p + (prev - p) * mu_ref[...])
        xs = jnp.concatenate(xs, axis=0) if n_rows > 1 else xs[0]
        r, k, v = xs[:, :W], xs[:, W:2 * W], xs[:, 2 * W:3 * W]
        wd = xs[:, 3 * W:3 * W + RW_DECAY_LORA]
        ad = xs[:, 3 * W + RW_DECAY_LORA:3 * W + RW_DECAY_LORA + RW_A_LORA]
        gd = xs[:, 3 * W + RW_DECAY_LORA + RW_A_LORA:]

        w_log = -_softplus(-(w0_ref[...] + _bdot(jnp.tanh(wd), wb_ref[...]))) - 0.5
        ld = -jnp.exp(w_log) * LOG2_E
        a_icl = jax.nn.sigmoid(a0_ref[...] + _bdot(ad, ab_ref[...]))
        gate = _bdot(jax.nn.sigmoid(gd), gb_ref[...])
        kk = k * kk_ref[...]
        kk = kk / jnp.maximum(jnp.sqrt(_head_sums(kk * kk)), 1e-12)
        k2 = k * (1.0 + (a_icl - 1.0) * ka_ref[...])
        av = -kk
        bv = kk * a_icl

        lc = _cumdot(cum_mask, ld)
        llast = [lc[ci * C + C - 1:ci * C + C, :] for ci in cr]
        to_end = jnp.concatenate([llast[ci] - lc[ci * C:(ci + 1) * C] for ci in cr], axis=0)
        e_in = jnp.exp2(lc)
        e_inv = jnp.exp2(-lc)
        e_end = jnp.exp2(to_end)
        a_t = av * jnp.exp2(lc - ld)
        r_t = r * e_in
        b_t = bv * e_inv
        k_t = k2 * e_inv
        b_h = bv * e_end
        k_h = k2 * e_end
        g_chunk = [jnp.exp2(llast[ci]) for ci in cr]

        g = [_bdot(jnp.concatenate([part(a_t, u), part(r_t, u)], axis=0),
                   jnp.concatenate([part(b_t, u), part(k_t, u)], axis=0), NT) for u in units]
        l_pow = [jnp.where(strict, g[n][:C, :C], 0.0).astype(BF16) for n in ur]
        a_rb = [jnp.where(incl, g[n][C:, :C], 0.0).astype(BF16) for n in ur]
        vh = [part(v, u).astype(BF16) for u in units]
        kv = [_bdot(jnp.concatenate([jnp.where(strict, g[n][:C, C:], 0.0),
                                     jnp.where(incl, g[n][C:, C:], 0.0)], axis=0), vh[n]) for n in ur]
        au = [jnp.concatenate([part(a_t, units[n]), kv[n][:C]], axis=1) for n in ur]
        span = 1
        while 2 * span < C:
            res = [_bdot(l_pow[n], jnp.concatenate([au[n].astype(BF16), l_pow[n]], axis=1)) for n in ur]
            au = [au[n] + res[n][:, :2 * N] for n in ur]
            l_pow = [res[n][:, 2 * N:].astype(BF16) for n in ur]
            span *= 2
        au = [au[n] + _bdot(l_pow[n], au[n]) for n in ur]
        mn = [_bdot(au[n], part(b_h, units[n]), TN) for n in ur]
        nt = [mn[n][N:] + _bdot(vh[n], part(k_h, units[n]), TN) for n in ur]
        ry = [_bdot(a_rb[n], au[n]) for n in ur]
        y0 = [ry[n][:, N:] + kv[n][C:] for n in ur]
        r_hat = [part(r_t, units[n]) + ry[n][:, :N] for n in ur]
        ys = []
        for bi in range(n_rows):
            s_cur = [s_scr[bi, h] for h in hr]
            for ci in range(bi * group, (bi + 1) * group):
                base = ci * RW_HEADS
                ys.append([_bdot(r_hat[base + h], s_cur[h], NT) + y0[base + h] for h in hr])
                s_cur = [s_cur[h] * g_chunk[ci][:, heads[h]] + _bdot(s_cur[h], mn[base + h][:N]) + nt[base + h]
                         for h in hr]
            for h in hr:
                s_scr[bi, h] = s_cur[h]

        y = jnp.concatenate([jnp.concatenate(ys[ci], axis=1) for ci in cr], axis=0)
        mean = _head_sums(y) * (1.0 / N)
        yc = y - mean
        var = _head_sums(yc * yc) * (1.0 / N)
        yn = yc * lax.rsqrt(var + GN_EPS) * gnw_ref[...] + gnb_ref[...]
        bonus = _head_sums(r * k2 * rk_ref[...]) * v
        out = ((yn + bonus) * gate).astype(o_ref.dtype)
        for bi in range(n_rows):
            o_ref[bi, pl.ds(off, R), :] = out[bi * R:(bi + 1) * R]
        return carry

    lax.fori_loop(0, n_chunks // group, one_group, 0)

    @pl.when(pl.program_id(1) == pl.num_programs(1) - 1)
    def _():
        sn_ref[...] = s_scr[...]
        shn_ref[...] = sh_scr[...]


def _rwkv(p, s0, sh0, vecs, wb, ab, gb):
    bsz, t_len, _ = p.shape
    chunk = min(TIME_CHUNK, t_len)
    t_blk = min(TIME_BLOCK, t_len)
    bb = _rows_per_step(bsz, t_len // chunk, RW_CHUNK_GROUP)
    mu, w0, a0, kk, ka, rk, gnw, gnb = vecs
    state_spec = pl.BlockSpec((bb, RW_HEADS, RW_HEAD, RW_HEAD), lambda b, c: (b, 0, 0, 0))
    shift_spec = pl.BlockSpec((bb, 1, RW_COLS), lambda b, c: (b, 0, 0))
    return pl.pallas_call(
        functools.partial(_rwkv_kernel, chunk=chunk),
        grid=(bsz // bb, t_len // t_blk),
        in_specs=[pl.BlockSpec((bb, t_blk, RW_COLS), lambda b, c: (b, c, 0)), state_spec, shift_spec,
                  _const_spec(mu.shape), _const_spec(w0.shape), _const_spec(wb.shape), _const_spec(a0.shape),
                  _const_spec(ab.shape), _const_spec(gb.shape), _const_spec(kk.shape), _const_spec(ka.shape),
                  _const_spec(rk.shape), _const_spec(gnw.shape), _const_spec(gnb.shape)],
        out_specs=[pl.BlockSpec((bb, t_blk, RW_WIDTH), lambda b, c: (b, c, 0)), state_spec, shift_spec],
        out_shape=[jax.ShapeDtypeStruct((bsz, t_len, RW_WIDTH), BF16),
                   jax.ShapeDtypeStruct((bsz, RW_HEADS, RW_HEAD, RW_HEAD), F32),
                   jax.ShapeDtypeStruct((bsz, 1, RW_COLS), F32)],
        scratch_shapes=[pltpu.VMEM((bb, RW_HEADS, RW_HEAD, RW_HEAD), F32), pltpu.VMEM((bb, 1, RW_COLS), F32)],
        compiler_params=_params(("parallel", "arbitrary")),
        name="rwkv",
    )(p, s0, sh0, mu, w0, wb, a0, ab, gb, kk, ka, rk, gnw, gnb)


def _cross_kernel(x_ref, a_ref, b_ref, k_ref, v_ref, wa_ref, wb_ref, g_ref, wq_ref, wo_ref, o_ref):
    bb, tq, d = x_ref.shape
    rows = bb * tq
    x = x_ref[...].reshape(rows, d)
    x = x + jnp.dot(a_ref[...].reshape(rows, a_ref.shape[2]), wa_ref[...], preferred_element_type=F32)
    x = x + jnp.dot(b_ref[...].reshape(rows, b_ref.shape[2]), wb_ref[...], preferred_element_type=F32)
    q = jnp.dot(_rmsnorm(x, g_ref[...]).astype(BF16), wq_ref[...], preferred_element_type=F32)
    heads = [slice(h * X_DIM, (h + 1) * X_DIM) for h in range(X_HEADS)]
    units = [(i, sl) for i in range(bb) for sl in heads]
    s = [_bdot(q[i * tq:(i + 1) * tq, sl], k_ref[i, :, sl], NT) * (X_DIM ** -0.5) for i, sl in units]
    e = [jnp.exp(sn - jnp.max(sn, axis=-1, keepdims=True)) for sn in s]
    pr = [en / jnp.sum(en, axis=-1, keepdims=True) for en in e]
    o = [_bdot(pr[n], v_ref[i, :, sl]).astype(BF16) for n, (i, sl) in enumerate(units)]
    per_row = [jnp.concatenate(o[i * X_HEADS:(i + 1) * X_HEADS], axis=1) for i in range(bb)]
    attn = jnp.concatenate(per_row, axis=0) if bb > 1 else per_row[0]
    out = x + jnp.dot(attn, wo_ref[...], preferred_element_type=F32)
    o_ref[...] = out.reshape(bb, tq, d)


def _cross(x, a, b, mk, mv, wa, wb, g, wq, wo):
    bsz, t_len, d = x.shape
    tq = min(CROSS_TILE, t_len)
    bb = min(bsz, TOKEN_TILE // min(tq, TOKEN_TILE))
    tok_spec = lambda width: pl.BlockSpec((bb, tq, width), lambda i, t: (i, t, 0))
    mem_spec = pl.BlockSpec((bb, N_MEM, d), lambda i, t: (i, 0, 0))
    return pl.pallas_call(
        _cross_kernel,
        grid=(bsz // bb, t_len // tq),
        in_specs=[tok_spec(d), tok_spec(a.shape[2]), tok_spec(b.shape[2]), mem_spec, mem_spec,
                  _const_spec(wa.shape), _const_spec(wb.shape), _const_spec((1, d)),
                  _const_spec(wq.shape), _const_spec(wo.shape)],
        out_specs=tok_spec(d),
        out_shape=jax.ShapeDtypeStruct(x.shape, F32),
        compiler_params=_params(("parallel", "parallel")),
        name="cross",
    )(x, a, b, mk, mv, wa, wb, g, wq, wo)


def _ffn_kernel(x_ref, g_ref, w1_ref, w3_ref, w2_ref, gf_ref, o_ref):
    x = x_ref[...]
    n = _rmsnorm(x, g_ref[...]).astype(BF16)
    acc = x
    for c0 in range(0, D_FF, FF_CHUNK):
        c1 = c0 + FF_CHUNK
        up = jnp.dot(n, w1_ref[:, c0:c1], preferred_element_type=F32)
        lin = jnp.dot(n, w3_ref[:, c0:c1], preferred_element_type=F32)
        hid = (jax.nn.silu(up) * lin).astype(BF16)
        acc = acc + jnp.dot(hid, w2_ref[c0:c1, :], preferred_element_type=F32)
    o_ref[...] = _rmsnorm(acc, gf_ref[...])


def _ffn(x, g, w1, w3, w2, gf):
    rows, d = x.shape
    tm = min(FFN_TILE, rows)
    row_spec = pl.BlockSpec((tm, d), lambda i: (i, 0))
    return pl.pallas_call(
        _ffn_kernel,
        grid=(rows // tm,),
        in_specs=[row_spec, _const_spec((1, d)), _const_spec(w1.shape), _const_spec(w3.shape),
                  _const_spec(w2.shape), _const_spec((1, d))],
        out_specs=row_spec,
        out_shape=jax.ShapeDtypeStruct((rows, d), F32),
        compiler_params=_params(("parallel",)),
        name="ffn",
    )(x, g, w1, w3, w2, gf)


def _trunk(x, mk, mv, hg0, rw0, sh0, lb, w):
    bsz, t_len, d = x.shape
    x2d = x.reshape(bsz * t_len, d)
    p_hg, p_rw = _norm_matmul(x2d, w['norm_mix'], [w['w_in_hg'], w['w_in_rw']], "in_proj")
    hg_o, hg_s = _hgrn(p_hg.reshape(bsz, t_len, HG_COLS), hg0, lb, w['hgrn_norm'])
    rw_o, rw_s, rw_sh = _rwkv(p_rw.reshape(bsz, t_len, RW_COLS), rw0, sh0, w['rw_vecs'],
                              w['rw_w_b'], w['rw_a_b'], w['rw_g_b'])
    x3d = _cross(x, hg_o, rw_o, mk, mv, w['w_out_hg'], w['w_out_rw'], w['norm_cross'], w['w_cq'], w['w_co'])
    y = _ffn(x3d.reshape(bsz * t_len, d), w['norm_ffn'], w['w_ff1'], w['w_ff3'], w['w_ff2'], w['norm_final'])
    return y.reshape(bsz, t_len, d), hg_s, rw_s, rw_sh


def kernel(x_prompt, mem_prompt, x_sample, cache_mem_k, cache_mem_v, state_hgrn, state_rwkv, state_rwkv_shift, hgrn_lb_logits, norm_mix, w_in, hgrn_norm, rw_mu, rw_w0, rw_w_b, rw_a0, rw_a_b, rw_g_b, rw_k_k, rw_k_a, rw_r_k, rw_gn_w, rw_gn_b, w_out, norm_cross, norm_mem, w_cq, w_ck, w_cv, w_co, norm_ffn, w_ff1, w_ff3, w_ff2, norm_final):
    assert w_in.shape[0] == 1, "single-layer trunk"
    bp = x_prompt.shape[0]
    row = lambda a: a.reshape(1, -1).astype(F32)
    bf = lambda a: a.astype(BF16)
    lb = jnp.cumsum(jax.nn.softmax(hgrn_lb_logits.astype(F32), axis=0), axis=0)[0:1]
    w = {
        'norm_mix': row(norm_mix[0]), 'w_in_hg': bf(w_in[0, :, :HG_COLS]), 'w_in_rw': bf(w_in[0, :, HG_COLS:]),
        'hgrn_norm': row(hgrn_norm[0]),
        'rw_vecs': tuple(row(a[0]) for a in (rw_mu, rw_w0, rw_a0, rw_k_k, rw_k_a, rw_r_k, rw_gn_w, rw_gn_b)),
        'rw_w_b': bf(rw_w_b[0]), 'rw_a_b': bf(rw_a_b[0]), 'rw_g_b': bf(rw_g_b[0]),
        'w_out_hg': bf(w_out[0, :HG_WIDTH]), 'w_out_rw': bf(w_out[0, HG_WIDTH:]),
        'norm_cross': row(norm_cross[0]), 'w_cq': bf(w_cq[0]), 'w_co': bf(w_co[0]),
        'norm_ffn': row(norm_ffn[0]), 'w_ff1': bf(w_ff1[0]), 'w_ff3': bf(w_ff3[0]), 'w_ff2': bf(w_ff2[0]),
        'norm_final': row(norm_final),
    }
    mk, mv = _norm_matmul(mem_prompt.reshape(bp * N_MEM, D_MODEL), row(norm_mem[0]),
                          [bf(w_ck[0]), bf(w_cv[0])], "mem_kv")
    mk = mk.reshape(bp, N_MEM, D_MODEL)
    mv = mv.reshape(bp, N_MEM, D_MODEL)
    y_p, p_hg, p_rw, p_sh = _trunk(
        x_prompt, mk, mv, jnp.zeros((bp, HG_HEADS, HG_DIM, HG_DIM), F32),
        jnp.zeros((bp, RW_HEADS, RW_HEAD, RW_HEAD), F32), jnp.zeros((bp, 1, RW_COLS), F32), lb, w)
    bs = x_sample.shape[0]
    y_s, s_hg, s_rw, s_sh = _trunk(
        x_sample, cache_mem_k[0].reshape(bs, N_MEM, D_MODEL), cache_mem_v[0].reshape(bs, N_MEM, D_MODEL),
        state_hgrn[0], state_rwkv[0], state_rwkv_shift[0], lb, w)
    return (y_p, y_s, p_hg[None], p_rw[None], p_sh[None],
            mk.reshape(1, bp, N_MEM, X_HEADS, X_DIM), mv.reshape(1, bp, N_MEM, X_HEADS, X_DIM),
            s_hg[None], s_rw[None], s_sh[None])
```

```python
import functools

import jax
import jax.numpy as jnp
from jax import lax
from jax.experimental import pallas as pl
from jax.experimental.pallas import tpu as pltpu

F32 = jnp.float32
BF16 = jnp.bfloat16

D_MODEL = 1024
HG_WIDTH = 512
HG_HEADS = 4
HG_DIM = 128
HG_BLOCK = 16
HG_COLS = 4 * HG_WIDTH
RW_WIDTH = 512
RW_HEAD = 64
RW_HEADS = 8
RW_DECAY_LORA = 64
RW_A_LORA = 64
RW_GATE_LORA = 128
RW_COLS = 3 * RW_WIDTH + RW_DECAY_LORA + RW_A_LORA + RW_GATE_LORA
N_MEM = 256
X_HEADS = 4
X_DIM = D_MODEL // X_HEADS
D_FF = 2816
RMS_EPS = 1e-6
GN_EPS = 64e-5
LOG2_E = 1.4426950408889634

KK_NORM_FLOOR = 1e-12

TIME_CHUNK = 64
TIME_BLOCK = 1024
HG_CHUNK_GROUP = 8
RW_CHUNK_GROUP = 4
TOKEN_TILE = 1024
CROSS_TILE = 1024
FFN_TILE = 1024
FF_CHUNK = 256
COL_CHUNK = 1024
VMEM_LIMIT_BYTES = 56 * 1024 * 1024

NN = (((1,), (0,)), ((), ()))
NT = (((1,), (1,)), ((), ()))
TN = (((0,), (0,)), ((), ()))


def _bdot(a, b, dims=NN):
    return lax.dot_general(a.astype(BF16), b.astype(BF16), dims, preferred_element_type=F32)


def _split3(x):
    hi = x.astype(BF16)
    r1 = x - hi.astype(F32)
    mid = r1.astype(BF16)
    lo = (r1 - mid.astype(F32)).astype(BF16)
    return hi, mid, lo


def _cumdot(mask3, x):
    return jnp.dot(mask3, jnp.concatenate(_split3(x), axis=0), preferred_element_type=F32)


def _head_sums(x):
    pair = 2 * RW_HEAD
    low = lax.broadcasted_iota(jnp.int32, (x.shape[0], pair), 1) < RW_HEAD
    out = []
    for c0 in range(0, x.shape[1], pair):
        xp = x[:, c0:c0 + pair]
        s_low = jnp.sum(jnp.where(low, xp, 0.0), axis=1, keepdims=True)
        s_high = jnp.sum(jnp.where(low, 0.0, xp), axis=1, keepdims=True)
        out.append(jnp.where(low, s_low, s_high))
    return jnp.concatenate(out, axis=1)


def _rmsnorm(x, g):
    return x * lax.rsqrt(jnp.mean(x * x, axis=-1, keepdims=True) + RMS_EPS) * g


def _softplus(x):
    return jnp.maximum(x, 0.0) + jnp.log(1.0 + jnp.exp(-jnp.abs(x)))


def _params(semantics):
    return pltpu.CompilerParams(dimension_semantics=semantics, vmem_limit_bytes=VMEM_LIMIT_BYTES)


def _const_spec(shape):
    zeros = (0,) * len(shape)
    return pl.BlockSpec(shape, lambda *_: zeros, pipeline_mode=pl.Buffered(1))


def _rows_per_step(bsz, chunks_per_row, group):
    if chunks_per_row >= group:
        return 1
    rows = max(1, group // chunks_per_row)
    while bsz % rows:
        rows -= 1
    return rows


def _norm_matmul_kernel(x_ref, g_ref, *refs):
    n_out = len(refs) // 2
    w_refs, o_refs = refs[:n_out], refs[n_out:]
    n = _rmsnorm(x_ref[...], g_ref[...]).astype(BF16)
    for w_ref, o_ref in zip(w_refs, o_refs):
        cols = w_ref.shape[1]
        for c0 in range(0, cols, COL_CHUNK):
            c1 = min(c0 + COL_CHUNK, cols)
            o_ref[:, c0:c1] = jnp.dot(n, w_ref[:, c0:c1], preferred_element_type=F32).astype(o_ref.dtype)


def _norm_matmul(x, g, ws, name):
    rows, k = x.shape
    tm = min(TOKEN_TILE, rows)
    return pl.pallas_call(
        _norm_matmul_kernel,
        grid=(rows // tm,),
        in_specs=[pl.BlockSpec((tm, k), lambda i: (i, 0)), _const_spec((1, k))]
        + [_const_spec(w.shape) for w in ws],
        out_specs=[pl.BlockSpec((tm, w.shape[1]), lambda i: (i, 0)) for w in ws],
        out_shape=[jax.ShapeDtypeStruct((rows, w.shape[1]), F32) for w in ws],
        compiler_params=_params(("parallel",)),
        name=name,
    )(x, g, *ws)


def _hgrn_kernel(p_ref, s0_ref, lb_ref, gn_ref, o_ref, sn_ref, st_scr, *, chunk):
    n_blk = chunk // HG_BLOCK
    W = HG_WIDTH
    heads = [slice(h * HG_DIM, (h + 1) * HG_DIM) for h in range(HG_HEADS)]
    hr = range(HG_HEADS)

    @pl.when(pl.program_id(1) == 0)
    def _():
        for bi in range(p_ref.shape[0]):
            for h in hr:
                st_scr[bi, h] = s0_ref[bi, h].T

    ti = lax.broadcasted_iota(jnp.int32, (chunk, chunk), 0)
    tj = lax.broadcasted_iota(jnp.int32, (chunk, chunk), 1)
    diag_mask = (ti >= tj) & (ti // HG_BLOCK == tj // HG_BLOCK)
    ri = lax.broadcasted_iota(jnp.int32, (3 * chunk, 3 * chunk), 0)
    ci = lax.broadcasted_iota(jnp.int32, (3 * chunk, 3 * chunk), 1) % chunk
    kind, rt = ri // chunk, ri % chunk
    same = rt // HG_BLOCK == ci // HG_BLOCK
    as_f32 = lambda m: m.astype(F32)
    cum_mask = jnp.where(kind == 0, as_f32(rt >= ci),
                         jnp.where(kind == 1, as_f32((rt >= ci) & same),
                                   as_f32(ci // HG_BLOCK <= rt // HG_BLOCK))).astype(BF16)
    n_rows = p_ref.shape[0]
    n_chunks = p_ref.shape[1] // chunk
    group = HG_CHUNK_GROUP if n_chunks % HG_CHUNK_GROUP == 0 else 1
    R = group * chunk
    cr = range(n_rows * group)
    units = [(c, h) for c in cr for h in hr]
    ur = range(len(units))
    lb = lb_ref[...]
    gn = gn_ref[...]

    def rows(x, c):
        return x[c * chunk:(c + 1) * chunk]

    def part(x, unit):
        c, h = unit
        return x[c * chunk:(c + 1) * chunk, heads[h]]

    def one_group(i, carry):
        off = pl.multiple_of(i * R, R)
        p = [p_ref[bi, pl.ds(off, R), :] for bi in range(n_rows)]
        p = jnp.concatenate(p, axis=0) if n_rows > 1 else p[0]
        q, fpre, v, gate = p[:, :W], p[:, W:2 * W], p[:, 2 * W:3 * W], p[:, 3 * W:]
        sig = jax.nn.sigmoid(fpre)
        logf = jnp.log(lb + (1.0 - lb) * sig) * LOG2_E
        k = (1.0 - lb) * (1.0 - sig)
        cums = [_cumdot(cum_mask, rows(logf, c)) for c in cr]
        stack = lambda pieces: jnp.concatenate(pieces, axis=0) if len(pieces) > 1 else pieces[0]
        lc = stack([cums[c][:chunk] for c in cr])
        lblk = stack([cums[c][chunk:2 * chunk] for c in cr])
        lend = stack([cums[c][2 * chunk:] for c in cr])
        llast = [cums[c][chunk - 1:chunk, :] for c in cr]
        to_end = stack([llast[c] - cums[c][:chunk] for c in cr])

        q_in = q * jnp.exp2(lc)
        qg = q * jnp.exp2(lblk)
        kg = k * jnp.exp2(-lblk)
        kd = (k * jnp.exp2(lend - lc)).astype(BF16)
        k_end = k * jnp.exp2(to_end)
        g_chunk = [jnp.exp2(llast[c]) for c in cr]
        q_cross, k_cross = [], []
        for j in range(n_blk - 1):
            lo, hi = j * HG_BLOCK, (j + 1) * HG_BLOCK
            q_rows, k_rows = [], []
            for c in cr:
                end_j = cums[c][2 * chunk + lo:2 * chunk + lo + 1, :]
                later = slice(c * chunk + hi, (c + 1) * chunk)
                q_rows += [jnp.zeros((hi, W), BF16), (q[later] * jnp.exp2(lc[later] - end_j)).astype(BF16)]
                k_rows += [jnp.zeros((lo, W), BF16)] if lo else []
                k_rows += [kd[c * chunk + lo:c * chunk + hi], jnp.zeros((chunk - hi, W), BF16)]
            q_cross.append(jnp.concatenate(q_rows, axis=0))
            k_cross.append(jnp.concatenate(k_rows, axis=0))

        att = [jnp.where(diag_mask, _bdot(part(qg, u), part(kg, u), NT), 0.0) for u in units]
        if n_blk > 1:
            att = [att[n] + _bdot(jnp.concatenate([part(qc, units[n]) for qc in q_cross], axis=1),
                                  jnp.concatenate([part(kc, units[n]) for kc in k_cross], axis=1), NT) for n in ur]
        o_intra = [_bdot(att[n], part(v, units[n])) for n in ur]
        kv = [_bdot(part(v, u), part(k_end, u), TN) for u in units]
        for bi in range(n_rows):
            st = [st_scr[bi, h] for h in hr]
            for c in range(bi * group, (bi + 1) * group):
                base = c * HG_HEADS
                o = [_bdot(part(q_in, (c, h)), st[h], NT) + o_intra[base + h] for h in hr]
                st = [st[h] * g_chunk[c][:, heads[h]] + kv[base + h] for h in hr]
                for h in hr:
                    on = o[h] * lax.rsqrt(jnp.mean(o[h] * o[h], axis=-1, keepdims=True) + RMS_EPS)
                    o_ref[bi, pl.ds(off + (c - bi * group) * chunk, chunk), heads[h]] = (
                        on * gn[:, heads[h]] * jax.nn.silu(part(gate, (c, h)))).astype(o_ref.dtype)
            for h in hr:
                st_scr[bi, h] = st[h]
        return carry

    lax.fori_loop(0, n_chunks // group, one_group, 0)

    @pl.when(pl.program_id(1) == pl.num_programs(1) - 1)
    def _():
        for bi in range(n_rows):
            for h in hr:
                sn_ref[bi, h] = st_scr[bi, h].T


def _hgrn(p, s0, lb, gn):
    bsz, t_len, _ = p.shape
    chunk = min(TIME_CHUNK, t_len)
    t_blk = min(TIME_BLOCK, t_len)
    bb = _rows_per_step(bsz, t_len // chunk, HG_CHUNK_GROUP)
    state_spec = pl.BlockSpec((bb, HG_HEADS, HG_DIM, HG_DIM), lambda b, c: (b, 0, 0, 0))
    return pl.pallas_call(
        functools.partial(_hgrn_kernel, chunk=chunk),
        grid=(bsz // bb, t_len // t_blk),
        in_specs=[pl.BlockSpec((bb, t_blk, HG_COLS), lambda b, c: (b, c, 0)), state_spec,
                  _const_spec((1, HG_WIDTH)), _const_spec((1, HG_WIDTH))],
        out_specs=[pl.BlockSpec((bb, t_blk, HG_WIDTH), lambda b, c: (b, c, 0)), state_spec],
        out_shape=[jax.ShapeDtypeStruct((bsz, t_len, HG_WIDTH), BF16),
                   jax.ShapeDtypeStruct((bsz, HG_HEADS, HG_DIM, HG_DIM), F32)],
        scratch_shapes=[pltpu.VMEM((bb, HG_HEADS, HG_DIM, HG_DIM), F32)],
        compiler_params=_params(("parallel", "arbitrary")),
        name="hgrn",
    )(p, s0, lb, gn)


def _rwkv_kernel(p_ref, s0_ref, sh0_ref, mu_ref, w0_ref, wb_ref, a0_ref, ab_ref, gb_ref, kk_ref, ka_ref,
                 rk_ref, gnw_ref, gnb_ref, o_ref, sn_ref, shn_ref, s_scr, sh_scr, *, chunk):
    W, N, C = RW_WIDTH, RW_HEAD, chunk
    heads = [slice(h * N, (h + 1) * N) for h in range(RW_HEADS)]
    hr = range(RW_HEADS)

    @pl.when(pl.program_id(1) == 0)
    def _():
        s_scr[...] = s0_ref[...]
        sh_scr[...] = sh0_ref[...]

    n_rows = p_ref.shape[0]
    n_chunks = p_ref.shape[1] // C
    group = RW_CHUNK_GROUP if n_chunks % RW_CHUNK_GROUP == 0 else 1
    R = group * C
    RR = n_rows * R
    cr = range(n_rows * group)
    units = [(ci, h) for ci in cr for h in hr]
    ur = range(len(units))
    row = lax.broadcasted_iota(jnp.int32, (R, 1), 0)
    ti = lax.broadcasted_iota(jnp.int32, (C, C), 0)
    tj = lax.broadcasted_iota(jnp.int32, (C, C), 1)
    incl = ti >= tj
    strict = ti > tj
    mi = lax.broadcasted_iota(jnp.int32, (RR, 3 * RR), 0)
    mj = lax.broadcasted_iota(jnp.int32, (RR, 3 * RR), 1) % RR
    cum_mask = ((mi >= mj) & (mi // C == mj // C)).astype(BF16)

    def part(x, unit):
        ci, h = unit
        return x[ci * C:(ci + 1) * C, heads[h]]

    def one_group(i, carry):
        off = pl.multiple_of(i * R, R)
        xs = []
        for bi in range(n_rows):
            p = p_ref[bi, pl.ds(off, R), :]
            prev = jnp.where(row == 0, sh_scr[bi], pltpu.roll(p, 1, 0))
            sh_scr[bi] = p[R - 1:R, :]
            xs.append(p + (prev - p) * mu_ref[...])
        xs = jnp.concatenate(xs, axis=0) if n_rows > 1 else xs[0]
        r, k, v = xs[:, :W], xs[:, W:2 * W], xs[:, 2 * W:3 * W]
        wd = xs[:, 3 * W:3 * W + RW_DECAY_LORA]
        ad = xs[:, 3 * W + RW_DECAY_LORA:3 * W + RW_DECAY_LORA + RW_A_LORA]
        gd = xs[:, 3 * W + RW_DECAY_LORA + RW_A_LORA:]

        w_log = -_softplus(-(w0_ref[...] + _bdot(jnp.tanh(wd), wb_ref[...]))) - 0.5
        ld = -jnp.exp(w_log) * LOG2_E
        a_icl = jax.nn.sigmoid(a0_ref[...] + _bdot(ad, ab_ref[...]))
        gate = _bdot(jax.nn.sigmoid(gd), gb_ref[...])
        kk = k * kk_ref[...]
        kk = kk / jnp.maximum(jnp.sqrt(_head_sums(kk * kk)), KK_NORM_FLOOR)
        k2 = k * (1.0 + (a_icl - 1.0) * ka_ref[...])
        av = -kk
        bv = kk * a_icl

        lc = _cumdot(cum_mask, ld)
        llast = [lc[ci * C + C - 1:ci * C + C, :] for ci in cr]
        to_end = jnp.concatenate([llast[ci] - lc[ci * C:(ci + 1) * C] for ci in cr], axis=0)
        e_in = jnp.exp2(lc)
        e_inv = jnp.exp2(-lc)
        e_end = jnp.exp2(to_end)
        a_t = av * jnp.exp2(lc - ld)
        r_t = r * e_in
        b_t = bv * e_inv
        k_t = k2 * e_inv
        b_h = bv * e_end
        k_h = k2 * e_end
        g_chunk = [jnp.exp2(llast[ci]) for ci in cr]

        g = [_bdot(jnp.concatenate([part(a_t, u), part(r_t, u)], axis=0),
                   jnp.concatenate([part(b_t, u), part(k_t, u)], axis=0), NT) for u in units]
        l_pow = [jnp.where(strict, g[n][:C, :C], 0.0).astype(BF16) for n in ur]
        a_rb = [jnp.where(incl, g[n][C:, :C], 0.0).astype(BF16) for n in ur]
        vh = [part(v, u).astype(BF16) for u in units]
        kv = [_bdot(jnp.concatenate([jnp.where(strict, g[n][:C, C:], 0.0),
                                     jnp.where(incl, g[n][C:, C:], 0.0)], axis=0), vh[n]) for n in ur]
        au = [jnp.concatenate([part(a_t, units[n]), kv[n][:C]], axis=1) for n in ur]
        span = 1
        while 2 * span < C:
            res = [_bdot(l_pow[n], jnp.concatenate([au[n].astype(BF16), l_pow[n]], axis=1)) for n in ur]
            au = [au[n] + res[n][:, :2 * N] for n in ur]
            l_pow = [res[n][:, 2 * N:].astype(BF16) for n in ur]
            span *= 2
        au = [au[n] + _bdot(l_pow[n], au[n]) for n in ur]
        mn = [_bdot(au[n], part(b_h, units[n]), TN) for n in ur]
        nt = [mn[n][N:] + _bdot(vh[n], part(k_h, units[n]), TN) for n in ur]
        ry = [_bdot(a_rb[n], au[n]) for n in ur]
        y0 = [ry[n][:, N:] + kv[n][C:] for n in ur]
        r_hat = [part(r_t, units[n]) + ry[n][:, :N] for n in ur]
        ys = []
        for bi in range(n_rows):
            s_cur = [s_scr[bi, h] for h in hr]
            for ci in range(bi * group, (bi + 1) * group):
                base = ci * RW_HEADS
                ys.append([_bdot(r_hat[base + h], s_cur[h], NT) + y0[base + h] for h in hr])
                s_cur = [s_cur[h] * g_chunk[ci][:, heads[h]] + _bdot(s_cur[h], mn[base + h][:N]) + nt[base + h]
                         for h in hr]
            for h in hr:
                s_scr[bi, h] = s_cur[h]

        y = jnp.concatenate([jnp.concatenate(ys[ci], axis=1) for ci in cr], axis=0)
        mean = _head_sums(y) * (1.0 / N)
        yc = y - mean
        var = _head_sums(yc * yc) * (1.0 / N)
        yn = yc * lax.rsqrt(var + GN_EPS) * gnw_ref[...] + gnb_ref[...]
        bonus = _head_sums(r * k2 * rk_ref[...]) * v
        out = ((yn + bonus) * gate).astype(o_ref.dtype)
        for bi in range(n_rows):
            o_ref[bi, pl.ds(off, R), :] = out[bi * R:(bi + 1) * R]
        return carry

    lax.fori_loop(0, n_chunks // group, one_group, 0)

    @pl.when(pl.program_id(1) == pl.num_programs(1) - 1)
    def _():
        sn_ref[...] = s_scr[...]
        shn_ref[...] = sh_scr[...]


def _rwkv(p, s0, sh0, vecs, wb, ab, gb):
    bsz, t_len, _ = p.shape
    chunk = min(TIME_CHUNK, t_len)
    t_blk = min(TIME_BLOCK, t_len)
    bb = _rows_per_step(bsz, t_len // chunk, RW_CHUNK_GROUP)
    mu, w0, a0, kk, ka, rk, gnw, gnb = vecs
    state_spec = pl.BlockSpec((bb, RW_HEADS, RW_HEAD, RW_HEAD), lambda b, c: (b, 0, 0, 0))
    shift_spec = pl.BlockSpec((bb, 1, RW_COLS), lambda b, c: (b, 0, 0))
    return pl.pallas_call(
        functools.partial(_rwkv_kernel, chunk=chunk),
        grid=(bsz // bb, t_len // t_blk),
        in_specs=[pl.BlockSpec((bb, t_blk, RW_COLS), lambda b, c: (b, c, 0)), state_spec, shift_spec,
                  _const_spec(mu.shape), _const_spec(w0.shape), _const_spec(wb.shape), _const_spec(a0.shape),
                  _const_spec(ab.shape), _const_spec(gb.shape), _const_spec(kk.shape), _const_spec(ka.shape),
                  _const_spec(rk.shape), _const_spec(gnw.shape), _const_spec(gnb.shape)],
        out_specs=[pl.BlockSpec((bb, t_blk, RW_WIDTH), lambda b, c: (b, c, 0)), state_spec, shift_spec],
        out_shape=[jax.ShapeDtypeStruct((bsz, t_len, RW_WIDTH), BF16),
                   jax.ShapeDtypeStruct((bsz, RW_HEADS, RW_HEAD, RW_HEAD), F32),
                   jax.ShapeDtypeStruct((bsz, 1, RW_COLS), F32)],
        scratch_shapes=[pltpu.VMEM((bb, RW_HEADS, RW_HEAD, RW_HEAD), F32), pltpu.VMEM((bb, 1, RW_COLS), F32)],
        compiler_params=_params(("parallel", "arbitrary")),
        name="rwkv",
    )(p, s0, sh0, mu, w0, wb, a0, ab, gb, kk, ka, rk, gnw, gnb)


def _cross_kernel(x_ref, a_ref, b_ref, k_ref, v_ref, wa_ref, wb_ref, g_ref, wq_ref, wo_ref, o_ref):
    bb, tq, d = x_ref.shape
    rows = bb * tq
    x = x_ref[...].reshape(rows, d)
    x = x + jnp.dot(a_ref[...].reshape(rows, a_ref.shape[2]), wa_ref[...], preferred_element_type=F32)
    x = x + jnp.dot(b_ref[...].reshape(rows, b_ref.shape[2]), wb_ref[...], preferred_element_type=F32)
    q = jnp.dot(_rmsnorm(x, g_ref[...]).astype(BF16), wq_ref[...], preferred_element_type=F32)
    heads = [slice(h * X_DIM, (h + 1) * X_DIM) for h in range(X_HEADS)]
    units = [(i, sl) for i in range(bb) for sl in heads]
    s = [_bdot(q[i * tq:(i + 1) * tq, sl], k_ref[i, :, sl], NT) * (X_DIM ** -0.5) for i, sl in units]
    e = [jnp.exp(sn - jnp.max(sn, axis=-1, keepdims=True)) for sn in s]
    pr = [en / jnp.sum(en, axis=-1, keepdims=True) for en in e]
    o = [_bdot(pr[n], v_ref[i, :, sl]).astype(BF16) for n, (i, sl) in enumerate(units)]
    per_row = [jnp.concatenate(o[i * X_HEADS:(i + 1) * X_HEADS], axis=1) for i in range(bb)]
    attn = jnp.concatenate(per_row, axis=0) if bb > 1 else per_row[0]
    out = x + jnp.dot(attn, wo_ref[...], preferred_element_type=F32)
    o_ref[...] = out.reshape(bb, tq, d)


def _cross(x, a, b, mk, mv, wa, wb, g, wq, wo):
    bsz, t_len, d = x.shape
    tq = min(CROSS_TILE, t_len)
    bb = min(bsz, TOKEN_TILE // min(tq, TOKEN_TILE))
    tok_spec = lambda width: pl.BlockSpec((bb, tq, width), lambda i, t: (i, t, 0))
    mem_spec = pl.BlockSpec((bb, N_MEM, d), lambda i, t: (i, 0, 0))
    return pl.pallas_call(
        _cross_kernel,
        grid=(bsz // bb, t_len // tq),
        in_specs=[tok_spec(d), tok_spec(a.shape[2]), tok_spec(b.shape[2]), mem_spec, mem_spec,
                  _const_spec(wa.shape), _const_spec(wb.shape), _const_spec((1, d)),
                  _const_spec(wq.shape), _const_spec(wo.shape)],
        out_specs=tok_spec(d),
        out_shape=jax.ShapeDtypeStruct(x.shape, F32),
        compiler_params=_params(("parallel", "parallel")),
        name="cross",
    )(x, a, b, mk, mv, wa, wb, g, wq, wo)


def _ffn_kernel(x_ref, g_ref, w1_ref, w3_ref, w2_ref, gf_ref, o_ref):
    x = x_ref[...]
    n = _rmsnorm(x, g_ref[...]).astype(BF16)
    acc = x
    for c0 in range(0, D_FF, FF_CHUNK):
        c1 = c0 + FF_CHUNK
        up = jnp.dot(n, w1_ref[:, c0:c1], preferred_element_type=F32)
        lin = jnp.dot(n, w3_ref[:, c0:c1], preferred_element_type=F32)
        hid = (jax.nn.silu(up) * lin).astype(BF16)
        acc = acc + jnp.dot(hid, w2_ref[c0:c1, :], preferred_element_type=F32)
    o_ref[...] = _rmsnorm(acc, gf_ref[...])


def _ffn(x, g, w1, w3, w2, gf):
    rows, d = x.shape
    tm = min(FFN_TILE, rows)
    row_spec = pl.BlockSpec((tm, d), lambda i: (i, 0))
    return pl.pallas_call(
        _ffn_kernel,
        grid=(rows // tm,),
        in_specs=[row_spec, _const_spec((1, d)), _const_spec(w1.shape), _const_spec(w3.shape),
                  _const_spec(w2.shape), _const_spec((1, d))],
        out_specs=row_spec,
        out_shape=jax.ShapeDtypeStruct((rows, d), F32),
        compiler_params=_params(("parallel",)),
        name="ffn",
    )(x, g, w1, w3, w2, gf)


def _trunk(x, mk, mv, hg0, rw0, sh0, lb, w):
    bsz, t_len, d = x.shape
    x2d = x.reshape(bsz * t_len, d)
    p_hg, p_rw = _norm_matmul(x2d, w['norm_mix'], [w['w_in_hg'], w['w_in_rw']], "in_proj")
    hg_o, hg_s = _hgrn(p_hg.reshape(bsz, t_len, HG_COLS), hg0, lb, w['hgrn_norm'])
    rw_o, rw_s, rw_sh = _rwkv(p_rw.reshape(bsz, t_len, RW_COLS), rw0, sh0, w['rw_vecs'],
                              w['rw_w_b'], w['rw_a_b'], w['rw_g_b'])
    x3d = _cross(x, hg_o, rw_o, mk, mv, w['w_out_hg'], w['w_out_rw'], w['norm_cross'], w['w_cq'], w['w_co'])
    y = _ffn(x3d.reshape(bsz * t_len, d), w['norm_ffn'], w['w_ff1'], w['w_ff3'], w['w_ff2'], w['norm_final'])
    return y.reshape(bsz, t_len, d), hg_s, rw_s, rw_sh


def kernel(x_prompt, mem_prompt, x_sample, cache_mem_k, cache_mem_v, state_hgrn, state_rwkv, state_rwkv_shift, hgrn_lb_logits, norm_mix, w_in, hgrn_norm, rw_mu, rw_w0, rw_w_b, rw_a0, rw_a_b, rw_g_b, rw_k_k, rw_k_a, rw_r_k, rw_gn_w, rw_gn_b, w_out, norm_cross, norm_mem, w_cq, w_ck, w_cv, w_co, norm_ffn, w_ff1, w_ff3, w_ff2, norm_final):
    assert w_in.shape[0] == 1, "single-layer trunk"
    bp = x_prompt.shape[0]
    row = lambda a: a.reshape(1, -1).astype(F32)
    bf = lambda a: a.astype(BF16)
    lb = jnp.cumsum(jax.nn.softmax(hgrn_lb_logits.astype(F32), axis=0), axis=0)[0:1]
    w = {
        'norm_mix': row(norm_mix[0]), 'w_in_hg': bf(w_in[0, :, :HG_COLS]), 'w_in_rw': bf(w_in[0, :, HG_COLS:]),
        'hgrn_norm': row(hgrn_norm[0]),
        'rw_vecs': tuple(row(a[0]) for a in (rw_mu, rw_w0, rw_a0, rw_k_k, rw_k_a, rw_r_k, rw_gn_w, rw_gn_b)),
        'rw_w_b': bf(rw_w_b[0]), 'rw_a_b': bf(rw_a_b[0]), 'rw_g_b': bf(rw_g_b[0]),
        'w_out_hg': bf(w_out[0, :HG_WIDTH]), 'w_out_rw': bf(w_out[0, HG_WIDTH:]),
        'norm_cross': row(norm_cross[0]), 'w_cq': bf(w_cq[0]), 'w_co': bf(w_co[0]),
        'norm_ffn': row(norm_ffn[0]), 'w_ff1': bf(w_ff1[0]), 'w_ff3': bf(w_ff3[0]), 'w_ff2': bf(w_ff2[0]),
        'norm_final': row(norm_final),
    }
    mk, mv = _norm_matmul(mem_prompt.reshape(bp * N_MEM, D_MODEL), row(norm_mem[0]),
                          [bf(w_ck[0]), bf(w_cv[0])], "mem_kv")
    mk = mk.reshape(bp, N_MEM, D_MODEL)
    mv = mv.reshape(bp, N_MEM, D_MODEL)
    y_p, p_hg, p_rw, p_sh = _trunk(
        x_prompt, mk, mv, jnp.zeros((bp, HG_HEADS, HG_DIM, HG_DIM), F32),
        jnp.zeros((bp, RW_HEADS, RW_HEAD, RW_HEAD), F32), jnp.zeros((bp, 1, RW_COLS), F32), lb, w)
    bs = x_sample.shape[0]
    y_s, s_hg, s_rw, s_sh = _trunk(
        x_sample, cache_mem_k[0].reshape(bs, N_MEM, D_MODEL), cache_mem_v[0].reshape(bs, N_MEM, D_MODEL),
        state_hgrn[0], state_rwkv[0], state_rwkv_shift[0], lb, w)
    return (y_p, y_s, p_hg[None], p_rw[None], p_sh[None],
            mk.reshape(1, bp, N_MEM, X_HEADS, X_DIM), mv.reshape(1, bp, N_MEM, X_HEADS, X_DIM),
            s_hg[None], s_rw[None], s_sh[None])
```
